```python
import math
import jax
import jax.numpy as jnp
from jax import lax
import numpy as np

D_MODEL = 2048
BATCH = 4
SEQ = 8192
DEPTH = 2

GRID_W = 64
CTX_LEN = 256
EPS = 1e-6
CHUNK = 64

A_HEADS = 4
A_KDIM = 128
A_VDIM = 128
A_KWIDTH = A_HEADS * A_KDIM
A_WIDTH = A_HEADS * A_VDIM
M_HEADS = 16
M_HEADDIM = 64
M_INNER = M_HEADS * M_HEADDIM
M_GROUPS = 2
M_STATE = 128
M_CONV = 5
M_XBC = M_INNER + 2 * M_GROUPS * M_STATE
P_WINDOWS = (2, 4, 8, 16)
P_GROUP = 128
P_WIDTH = len(P_WINDOWS) * P_GROUP
F_HEADS = 4
F_GROUP = 128
F_WIDTH = F_HEADS * F_GROUP
N_BRANCH = 4
D_FF = 5632
FFN_CONV = 3

IN_SIZES = (A_KWIDTH, A_WIDTH, A_KWIDTH, A_KWIDTH, A_WIDTH,
            M_INNER, M_XBC, M_HEADS, M_HEADS,
            P_WIDTH, F_WIDTH, N_BRANCH * D_MODEL)
IN_COLS = sum(IN_SIZES)
AM_COLS = sum(IN_SIZES[:9])
F32 = jnp.float32

kernel_name = "hybrid_hgrn2_ssd_pool_fourier_dit_block"


def _rmsnorm(x, w):
    xf = x.astype(F32)
    y = xf * lax.rsqrt(jnp.mean(xf * xf, axis=-1, keepdims=True) + EPS)
    return (y * w.astype(F32)).astype(x.dtype)


def _modulate(h, shift, scale):
    return h * (1 + scale) + shift


def _dwconv(x, w, b):
    k = w.shape[0]
    y = lax.conv_general_dilated(x, w[:, None, :].astype(x.dtype), (1,), [(k // 2, k // 2)],
                                 dimension_numbers=("NWC", "WIO", "NWC"),
                                 feature_group_count=x.shape[-1])
    return y + b.astype(x.dtype)


def _split_cols(u):
    offs, acc = [], 0
    for s in IN_SIZES[:-1]:
        acc += s
        if acc >= u.shape[-1]:
            break
        offs.append(acc)
    return jnp.split(u, offs, axis=-1)


def _chunked_scan(q, k, v, logf, s0):
    bn, t, h, _ = q.shape
    n = t // CHUNK
    scalar = logf.shape[-1] == 1

    def blocks(a):
        return jnp.moveaxis(a.astype(F32).reshape(bn, n, CHUNK, h, a.shape[-1]), 1, 0)

    causal = jnp.tril(jnp.ones((CHUNK, CHUNK), bool))[None, :, :, None, None]

    def step(s, blk):
        qb, kb, vb, gb = blk
        b = jnp.cumsum(gb, axis=1)
        seg = b[:, :, None] - b[:, None, :]
        decay = jnp.where(causal, jnp.exp(jnp.where(causal, seg, 0.0)), 0.0)
        if scalar:
            scores = jnp.einsum("bthd,bshd->btsh", qb, kb) * decay[..., 0]
        else:
            scores = jnp.einsum("bthd,bshd,btshd->btsh", qb, kb, decay)
        o = (jnp.einsum("btsh,bshv->bthv", scores, vb)
             + jnp.einsum("bthd,bhdv->bthv", qb * jnp.exp(b), s))
        b_last = b[:, -1]
        s = (jnp.exp(b_last)[..., None] * s
             + jnp.einsum("bshd,bshv->bhdv", kb * jnp.exp(b_last[:, None] - b), vb))
        return s, o

    s, o = lax.scan(step, s0.astype(F32), tuple(map(blocks, (q, k, v, logf))))
    o = jnp.moveaxis(o, 0, 1).reshape(bn, t, h, v.shape[-1])
    return o, s


def _ctx_then_latent(ctx_terms, lat_terms, reverse):
    flip = (lambda a: jnp.flip(a, axis=1)) if reverse else (lambda a: a)
    bn, _, h, dk = ctx_terms[0].shape
    dv = ctx_terms[2].shape[-1]
    s0 = jnp.zeros((bn, h, dk, dv), F32)
    o_ctx, s_ctx = _chunked_scan(*map(flip, ctx_terms), s0)
    o_lat, _ = _chunked_scan(*map(flip, lat_terms), s_ctx)
    return flip(o_ctx), flip(o_lat)


def _hgrn2_terms(q_raw, i_raw, f_raw, lb):
    bn, t, _ = q_raw.shape
    q = jax.nn.silu(q_raw).reshape(bn, t, A_HEADS, A_KDIM)
    v = i_raw.reshape(bn, t, A_HEADS, A_VDIM)
    fr = f_raw.astype(F32).reshape(bn, t, A_HEADS, A_KDIM)
    lb = lb.astype(F32).reshape(A_HEADS, A_KDIM)
    logf = jnp.logaddexp(jnp.log(lb), jnp.log1p(-lb) + jax.nn.log_sigmoid(fr))
    k = (1.0 - lb) * jax.nn.sigmoid(-fr)
    return q, k, v, logf


def _hgrn2_out(o, g, norm_w, w_br):
    bn, t = g.shape[:2]
    o = _rmsnorm(o, norm_w).reshape(bn, t, A_WIDTH).astype(g.dtype)
    return (o * jax.nn.silu(g)) @ w_br


def _ssd_xbc(xbc_raw, conv_w, conv_b):
    xbc = jax.nn.silu(_dwconv(xbc_raw, conv_w, conv_b))
    bn, t, _ = xbc.shape
    xm, bm, cm = jnp.split(xbc, [M_INNER, M_INNER + M_GROUPS * M_STATE], axis=-1)
    xm = xm.reshape(bn, t, M_HEADS, M_HEADDIM)

    def rep(a):
        return jnp.repeat(a.reshape(bn, t, M_GROUPS, M_STATE), M_HEADS // M_GROUPS, axis=2)

    return xm, rep(bm), rep(cm)


def _ssd_terms(xm, bm, cm, dt_raw, dt_bias, a_log):
    dt = jax.nn.softplus(dt_raw.astype(F32) + dt_bias.astype(F32))
    logf = (dt * -jnp.exp(a_log.astype(F32)))[..., None]
    return cm, bm, xm.astype(F32) * dt[..., None], logf


def _ssd_out(o, xm, z, d_skip, norm_w, w_br):
    bn, t = z.shape[:2]
    y = (o + d_skip.astype(F32)[:, None] * xm.astype(F32)).reshape(bn, t, M_INNER).astype(z.dtype)
    return _rmsnorm(y * jax.nn.silu(z), norm_w) @ w_br


def _window_mean(x, axis, w):
    n = x.shape[axis]
    left = w // 2
    right = w - 1 - left
    zero = jnp.zeros_like(lax.slice_in_dim(x, 0, 1, axis=axis))
    cs = jnp.concatenate([zero, jnp.cumsum(x, axis=axis)], axis=axis)
    idx = jnp.arange(n)
    hi = jnp.minimum(idx + right + 1, n)
    lo = jnp.maximum(idx - left, 0)
    shape = [1] * x.ndim
    shape[axis] = n
    cnt = (hi - lo).astype(x.dtype).reshape(shape)
    return (jnp.take(cs, hi, axis=axis) - jnp.take(cs, lo, axis=axis)) / cnt


def _pool_mix(p_in, group_w, scale, grid):
    bn, t, _ = p_in.shape
    z = p_in.astype(F32).reshape(bn, t, len(P_WINDOWS), P_GROUP)
    outs = []
    for gi, w in enumerate(P_WINDOWS):
        zg = z[:, :, gi]
        if grid:
            zz = zg.reshape(bn, t // GRID_W, GRID_W, P_GROUP)
            m = _window_mean(_window_mean(zz, 1, w), 2, w).reshape(bn, t, P_GROUP)
        else:
            m = _window_mean(zg, 1, w)
        outs.append(m - zg)
    y = jnp.stack(outs, axis=2)
    y = jnp.einsum("btgc,gcd->btgd", y, group_w.astype(F32)).reshape(bn, t, P_WIDTH)
    return (y * scale.astype(F32)).astype(p_in.dtype)


def _fourier_mix(f_in):
    bn, t, _ = f_in.shape
    z = f_in.astype(F32).reshape(bn, t, F_HEADS, F_GROUP)
    y = jnp.fft.fft2(z, axes=(1, 3), norm="ortho").real
    return y.reshape(bn, t, F_WIDTH).astype(f_in.dtype)


def _merge(ya, ym, p_in, f_in, gate_logits, grid, p_group_w, p_scale, w_br_p, w_br_f, w_out):
    yp = _pool_mix(p_in, p_group_w, p_scale, grid) @ w_br_p
    yf = _fourier_mix(f_in) @ w_br_f
    bn, t, _ = gate_logits.shape
    g = jax.nn.sigmoid(gate_logits.astype(F32)).astype(ya.dtype).reshape(bn, t, N_BRANCH, D_MODEL)
    y = g[:, :, 0] * ya + g[:, :, 1] * ym + g[:, :, 2] * yp + g[:, :, 3] * yf
    return y @ w_out


def _token_mixer(u, uc, lb, a_norm_w, w_br_a, m_conv_w, m_conv_b, m_dt_bias, m_a_log, m_d,
                 m_norm_w, w_br_m, p_group_w, p_scale, w_br_p, w_br_f, w_out):
    lat = _split_cols(u)
    cx = _split_cols(uc)
    a_cf, a_lf = _ctx_then_latent(_hgrn2_terms(cx[0], cx[1], cx[2], lb[0]),
                                  _hgrn2_terms(lat[0], lat[1], lat[2], lb[0]), False)
    a_cb, a_lb = _ctx_then_latent(_hgrn2_terms(cx[0], cx[1], cx[3], lb[1]),
                                  _hgrn2_terms(lat[0], lat[1], lat[3], lb[1]), True)
    xm, bm, cm = _ssd_xbc(lat[6], m_conv_w, m_conv_b)
    cxm, cbm, ccm = _ssd_xbc(cx[6], m_conv_w, m_conv_b)
    m_cf, m_lf = _ctx_then_latent(_ssd_terms(cxm, cbm, ccm, cx[7], m_dt_bias[0], m_a_log[0]),
                                  _ssd_terms(xm, bm, cm, lat[7], m_dt_bias[0], m_a_log[0]), False)
    m_cb, m_lb = _ctx_then_latent(_ssd_terms(cxm, cbm, ccm, cx[8], m_dt_bias[1], m_a_log[1]),
                                  _ssd_terms(xm, bm, cm, lat[8], m_dt_bias[1], m_a_log[1]), True)

    def branches(parts, o_a, o_m, x_m, grid):
        ya = _hgrn2_out(o_a, parts[4], a_norm_w, w_br_a)
        ym = _ssd_out(o_m, x_m, parts[5], m_d, m_norm_w, w_br_m)
        return _merge(ya, ym, parts[9], parts[10], parts[11], grid,
                      p_group_w, p_scale, w_br_p, w_br_f, w_out)

    y = branches(lat, a_lf + a_lb, m_lf + m_lb, xm, True)
    yc = branches(cx, a_cf + a_cb, m_cf + m_cb, cxm, False) if len(cx) == len(IN_SIZES) else None
    return y, yc


def _conv_ffn(h, w_up, conv_w, conv_b, w_down):
    a, b = jnp.split(_dwconv(h @ w_up, conv_w, conv_b), 2, axis=-1)
    return (jax.nn.silu(a) * b) @ w_down


def setup_inputs(seed: int = 0) -> dict:
    key = jax.random.key(seed)
    k = jax.random.split(key, 32)
    L, D = DEPTH, D_MODEL

    def nrm(i, shape, scale):
        return scale * jax.random.normal(k[i], shape, F32)

    dt0 = jnp.exp(jax.random.uniform(k[13], (L, 2, M_HEADS), F32, math.log(1e-3), math.log(1e-1)))
    return {
        "x": nrm(0, (BATCH, SEQ, D), 1.0),
        "c": nrm(1, (BATCH, D), 1.0),
        "ctx": nrm(2, (BATCH, CTX_LEN, D), 1.0),
        "c_ctx": nrm(3, (D,), 1.0),
        "w_ada": nrm(4, (L, D, 6 * D), 0.5 * D ** -0.5),
        "b_ada": nrm(5, (L, 6 * D), 0.01),
        "norm1_w": 1.0 + nrm(6, (L, D), 0.02),
        "w_in": nrm(7, (L, D, IN_COLS), D ** -0.5),
        "a_lb_logits": nrm(8, (L, 2, A_KWIDTH), 0.1),
        "a_norm_w": 1.0 + nrm(9, (L, A_VDIM), 0.02),
        "w_br_a": nrm(10, (L, A_WIDTH, D), A_WIDTH ** -0.5),
        "m_conv_w": nrm(11, (L, M_CONV, M_XBC), M_CONV ** -0.5),
        "m_conv_b": nrm(12, (L, M_XBC), 0.01),
        "m_dt_bias": dt0 + jnp.log(-jnp.expm1(-dt0)),
        "m_a_log": jnp.log(jax.random.uniform(k[14], (L, 2, M_HEADS), F32, 1.0, 16.0)),
        "m_d": 1.0 + nrm(15, (L, M_HEADS), 0.02),
        "m_norm_w": 1.0 + nrm(16, (L, M_INNER), 0.02),
        "w_br_m": nrm(17, (L, M_INNER, D), M_INNER ** -0.5),
        "p_group_w": nrm(18, (L, len(P_WINDOWS), P_GROUP, P_GROUP), P_GROUP ** -0.5),
        "p_scale": 1.0 + nrm(19, (L, P_WIDTH), 0.1),
        "w_br_p": nrm(20, (L, P_WIDTH, D), P_WIDTH ** -0.5),
        "w_br_f": nrm(21, (L, F_WIDTH, D), F_WIDTH ** -0.5),
        "w_out": nrm(22, (L, D, D), D ** -0.5),
        "norm2_w": 1.0 + nrm(23, (L, D), 0.02),
        "w_up": nrm(24, (L, D, 2 * D_FF), D ** -0.5),
        "ffn_conv_w": nrm(25, (L, FFN_CONV, 2 * D_FF), FFN_CONV ** -0.5),
        "ffn_conv_b": nrm(26, (L, 2 * D_FF), 0.01),
        "w_down": nrm(27, (L, D_FF, D), D_FF ** -0.5),
        "final_norm_w": 1.0 + nrm(28, (D,), 0.02),
    }


def reference(x, c, ctx, c_ctx, w_ada, b_ada, norm1_w, w_in, a_lb_logits, a_norm_w, w_br_a,
              m_conv_w, m_conv_b, m_dt_bias, m_a_log, m_d, m_norm_w, w_br_m, p_group_w, p_scale,
              w_br_p, w_br_f, w_out, norm2_w, w_up, ffn_conv_w, ffn_conv_b, w_down, final_norm_w):
    lb_all = jnp.cumsum(jax.nn.softmax(a_lb_logits.astype(F32), axis=0), axis=0)
    lb_all = lb_all - lb_all[0]
    s_lat = jax.nn.silu(c)
    s_ctx = jax.nn.silu(c_ctx)
    for l in range(DEPTH):
        last = l == DEPTH - 1
        mod = jnp.split((s_lat @ w_ada[l] + b_ada[l])[:, None, :], 6, axis=-1)
        cmod = jnp.split(s_ctx @ w_ada[l] + b_ada[l], 6, axis=-1)
        h = _modulate(_rmsnorm(x, norm1_w[l]), mod[0], mod[1])
        hc = _modulate(_rmsnorm(ctx, norm1_w[l]), cmod[0], cmod[1])
        w_in_ctx = w_in[l][:, :AM_COLS] if last else w_in[l]
        y, yc = _token_mixer(h @ w_in[l], hc @ w_in_ctx, lb_all[l], a_norm_w[l], w_br_a[l],
                             m_conv_w[l], m_conv_b[l], m_dt_bias[l], m_a_log[l], m_d[l],
                             m_norm_w[l], w_br_m[l], p_group_w[l], p_scale[l], w_br_p[l],
                             w_br_f[l], w_out[l])
        x = x + mod[2] * y
        x = x + mod[5] * _conv_ffn(_modulate(_rmsnorm(x, norm2_w[l]), mod[3], mod[4]),
                                   w_up[l], ffn_conv_w[l], ffn_conv_b[l], w_down[l])
        if not last:
            ctx = ctx + cmod[2] * yc
            ctx = ctx + cmod[5] * _conv_ffn(_modulate(_rmsnorm(ctx, norm2_w[l]), cmod[3], cmod[4]),
                                            w_up[l], ffn_conv_w[l], ffn_conv_b[l], w_down[l])
    return _rmsnorm(x, final_norm_w)
```

```python
import functools
import math

import numpy as np
import jax
import jax.numpy as jnp
from jax import lax
from jax.experimental import pallas as pl
from jax.experimental.pallas import tpu as pltpu

F32 = jnp.float32
BF16 = jnp.bfloat16

D_MODEL = 2048
BATCH = 4
T_LAT = 8192
T_CTX = 256
R_TOT = T_LAT + T_CTX
DEPTH = 2
GRID_W = 64
EPS = 1e-6

A_HEADS = 4
A_DIM = 128
A_WIDTH = A_HEADS * A_DIM
M_HEADS = 16
M_HEADDIM = 64
M_INNER = M_HEADS * M_HEADDIM
M_GROUPS = 2
M_STATE = 128
M_CONV = 5
M_XBC = M_INNER + 2 * M_GROUPS * M_STATE
P_WINDOWS = (2, 4, 8, 16)
P_GROUP = 128
P_WIDTH = 512
F_WIDTH = 512
N_BRANCH = 4
D_FF = 5632
FFN_CONV = 3
AM_COLS = 5 * A_WIDTH + M_INNER + M_XBC + 2 * M_HEADS

U_Q, U_I, U_FF, U_FB, U_G = 0, 512, 1024, 1536, 2048
U_Z = 2560
U_XBC = 3584
U_DT = 5120
U_P = 5632
U_F = 6144
U_GATE = 6656
NU = U_GATE + N_BRANCH * D_MODEL
ACT_W = A_WIDTH + M_INNER + P_WIDTH + F_WIDTH

LANE = 128
SUBLANE = 8
CHUNK = 128
VMEM_LIMIT = 56 * 1024 * 1024


def _cp(sem):
    return pltpu.CompilerParams(dimension_semantics=sem, vmem_limit_bytes=VMEM_LIMIT)


def _sigmoid(x):
    return 1.0 / (1.0 + jnp.exp(-x))


def _silu(x):
    return x * _sigmoid(x)


def _softplus(x):
    return jnp.maximum(x, 0.0) + jnp.log(1.0 + jnp.exp(-jnp.abs(x)))


def _dot(a, b):
    return jnp.dot(a, b, preferred_element_type=F32)


def _dot_nt(a, b):
    return lax.dot_general(a, b, (((1,), (1,)), ((), ())), preferred_element_type=F32)


def _dot_exact(a, b):
    return jnp.dot(a, b, preferred_element_type=F32, precision=lax.Precision.HIGHEST)


def _split_bf16(x):
    hi = x.astype(BF16)
    lo = (x - hi.astype(F32)).astype(BF16)
    return hi, lo


def _row_is_ctx(row0, tm, t_lat):
    row = row0 + lax.broadcasted_iota(jnp.int32, (tm, 1), 0)
    return row >= t_lat


def _ada_kernel(c_ref, w_ref, b_ref, o_ref):
    s = _silu(c_ref[...])
    o_ref[0] = _dot(s.astype(BF16), w_ref[0].astype(BF16)) + b_ref[0]


def _ada(c_all, w_ada, b_ada, tn=1024):
    nl, d, n6 = w_ada.shape
    return pl.pallas_call(
        _ada_kernel,
        grid=(nl, n6 // tn),
        in_specs=[
            pl.BlockSpec((SUBLANE, d), lambda l, j: (0, 0)),
            pl.BlockSpec((1, d, tn), lambda l, j: (l, 0, j)),
            pl.BlockSpec((1, 1, tn), lambda l, j: (l, 0, j)),
        ],
        out_specs=pl.BlockSpec((1, SUBLANE, tn), lambda l, j: (l, 0, j)),
        out_shape=jax.ShapeDtypeStruct((nl, SUBLANE, n6), F32),
        compiler_params=_cp(("arbitrary", "arbitrary")),
        name="ada",
    )(c_all, w_ada, b_ada.reshape(nl, 1, n6))


def _nmm_kernel(x_ref, nw_ref, shl_ref, scl_ref, shc_ref, scc_ref, w_ref, o_ref, h_ref, *, tm, t_lat):
    i = pl.program_id(1)

    @pl.when(pl.program_id(2) == 0)
    def _():
        x = x_ref[0]
        ms = jnp.mean(x * x, axis=-1, keepdims=True)
        y = x * lax.rsqrt(ms + EPS) * nw_ref[...]
        is_ctx = _row_is_ctx(i * tm, tm, t_lat)
        scale = jnp.where(is_ctx, scc_ref[...], scl_ref[0])
        shift = jnp.where(is_ctx, shc_ref[...], shl_ref[0])
        h_ref[...] = (y * (1.0 + scale) + shift).astype(BF16)

    o_ref[0] = _dot(h_ref[...], w_ref[...]).astype(o_ref.dtype)


def _norm_mod_matmul(x, nw, shl, scl, shc, scc, w, *, t_lat, tm, tn, out_dtype=F32):
    b, r, d = x.shape
    n = w.shape[1]
    vec_l = pl.BlockSpec((1, 1, d), lambda bb, i, j: (bb, 0, 0))
    vec_c = pl.BlockSpec((1, d), lambda bb, i, j: (0, 0))
    return pl.pallas_call(
        functools.partial(_nmm_kernel, tm=tm, t_lat=t_lat),
        grid=(b, r // tm, n // tn),
        in_specs=[
            pl.BlockSpec((1, tm, d), lambda bb, i, j: (bb, i, 0)),
            vec_c, vec_l, vec_l, vec_c, vec_c,
            pl.BlockSpec((d, tn), lambda bb, i, j: (0, j)),
        ],
        out_specs=pl.BlockSpec((1, tm, tn), lambda bb, i, j: (bb, i, j)),
        out_shape=jax.ShapeDtypeStruct((b, r, n), out_dtype),
        scratch_shapes=[pltpu.VMEM((tm, d), BF16)],
        compiler_params=_cp(("arbitrary", "arbitrary", "arbitrary")),
        name="norm_mod_matmul",
    )(x, nw, shl, scl, shc, scc, w)


def _dwconv_rows(ext_ref, prev, main, nxt, w, ksize, row0, t_lat, r_tot):
    tm = main.shape[0]
    ext_ref[0:SUBLANE, :] = prev
    ext_ref[SUBLANE:SUBLANE + tm, :] = main
    ext_ref[SUBLANE + tm:2 * SUBLANE + tm, :] = nxt
    t = row0 + lax.broadcasted_iota(jnp.int32, (tm, 1), 0)
    is_ctx = t >= t_lat
    lo = jnp.where(is_ctx, t_lat, 0)
    hi = jnp.where(is_ctx, r_tot, t_lat)
    acc = main * w[ksize // 2:ksize // 2 + 1, :]
    for k in range(ksize):
        dk = k - ksize // 2
        if dk == 0:
            continue
        sh = ext_ref[pl.ds(SUBLANE + dk, tm), :]
        n = t + dk
        acc = acc + jnp.where((n >= lo) & (n < hi), sh, 0.0) * w[k:k + 1, :]
    return acc


def _halo_specs(tm, tc, col_of, r_tot):
    nb = tm // SUBLANE
    last = r_tot // SUBLANE - 1
    prev = pl.BlockSpec((1, SUBLANE, tc), lambda b, i, j: (b, jnp.maximum(i * nb - 1, 0), col_of(j)))
    nxt = pl.BlockSpec((1, SUBLANE, tc), lambda b, i, j: (b, jnp.minimum((i + 1) * nb, last), col_of(j)))
    return prev, nxt


def _conv_silu_kernel(p_ref, m_ref, n_ref, w_ref, b_ref, o_ref, ext_ref, *, tm, ksize, t_lat, r_tot):
    i = pl.program_id(1)
    acc = _dwconv_rows(ext_ref, p_ref[0], m_ref[0], n_ref[0], w_ref[...], ksize, i * tm, t_lat, r_tot)
    o_ref[0] = _silu(acc + b_ref[...])


def _conv_silu(u, col0, width, w, bias, *, t_lat, tm, tc=512):
    b, r, _ = u.shape
    ksize = w.shape[0]
    cb = col0 // tc
    prev, nxt = _halo_specs(tm, tc, lambda j: cb + j, r)
    return pl.pallas_call(
        functools.partial(_conv_silu_kernel, tm=tm, ksize=ksize, t_lat=t_lat, r_tot=r),
        grid=(b, r // tm, width // tc),
        in_specs=[
            prev,
            pl.BlockSpec((1, tm, tc), lambda bb, i, j: (bb, i, cb + j)),
            nxt,
            pl.BlockSpec((ksize, tc), lambda bb, i, j: (0, j)),
            pl.BlockSpec((1, tc), lambda bb, i, j: (0, j)),
        ],
        out_specs=pl.BlockSpec((1, tm, tc), lambda bb, i, j: (bb, i, j)),
        out_shape=jax.ShapeDtypeStruct((b, r, width), F32),
        scratch_shapes=[pltpu.VMEM((tm + 2 * SUBLANE, tc), F32)],
        compiler_params=_cp(("arbitrary", "arbitrary", "arbitrary")),
        name="conv_silu",
    )(u, u, u, w, bias.reshape(1, width))


def _level_masks(c):
    t = np.arange(c)[:, None]
    s = np.arange(c)[None, :]
    ms = [(t == s)]
    h = 1
    while h < c:
        g = 2 * h
        ms.append((t // g == s // g) & ((t % g) >= h) & ((s % g) < h))
        h *= 2
    fwd = np.stack(ms).astype(np.float32)
    return np.stack([fwd, np.transpose(fwd, (0, 2, 1))])


def _hgrn_dir(q_raw, v, f_raw, lbp, masks_ref, d, st_ref, o_ref, rev, c, nheads, hd):
    width = nheads * hd
    llb, l1m, oml = lbp[0:1, :], lbp[1:2, :], lbp[2:3, :]
    q = _silu(q_raw)
    e = jnp.exp(-jnp.abs(f_raw))
    inv = 1.0 / (1.0 + e)
    k = oml * jnp.where(f_raw >= 0, e * inv, inv)
    a = l1m + jnp.minimum(f_raw, 0.0) - jnp.log(1.0 + e)
    logf = jnp.maximum(llb, a) + jnp.log(1.0 + jnp.exp(-jnp.abs(llb - a)))

    t_idx = lax.broadcasted_iota(jnp.int32, (c, 1), 0)
    p = logf
    tot = logf
    levels = []
    h = 1
    while h < c:
        up = (t_idx & h) != 0
        tgt = jnp.logical_not(up) if rev else up
        levels.append(jnp.exp(jnp.where(tgt, p, tot - p)))
        if h < SUBLANE:
            sib = jnp.where(up, pltpu.roll(tot, h, 0), pltpu.roll(tot, c - h, 0))
        else:
            t4 = tot.reshape(c // (2 * h), 2, h, width)
            sib = jnp.concatenate([t4[:, 1:2], t4[:, 0:1]], axis=1).reshape(c, width)
        p = p + jnp.where(tgt, sib, 0.0)
        tot = tot + sib
        h *= 2
    eb = jnp.exp(p)
    ek = jnp.exp(tot - p)
    etot = jnp.exp(tot[0:1, :])

    for hh in range(nheads):
        sl = slice(hh * hd, (hh + 1) * hd)
        qh, kh, vh = q[:, sl], k[:, sl], v[:, sl]
        a_mat = _dot_nt(qh.astype(BF16), kh.astype(BF16)) * masks_ref[d, 0]
        for li, lev in enumerate(levels):
            eh = lev[:, sl]
            a_mat = a_mat + _dot_nt((qh * eh).astype(BF16), (kh * eh).astype(BF16)) * masks_ref[d, li + 1]
        st = st_ref[d, hh]
        o = _dot(a_mat.astype(BF16), vh.astype(BF16))
        o = o + _dot_nt((qh * eb[:, sl]).astype(BF16), st.astype(BF16))
        o_ref[0, :, sl] = o
        khat = kh * ek[:, sl]
        st_ref[d, hh] = st * etot[:, sl] + _dot(vh.T.astype(BF16), khat.astype(BF16))


def _hgrn_kernel(qf_ref, if_ref, ff_ref, qb_ref, ib_ref, fb_ref, lbf_ref, lbb_ref, masks_ref,
                 of_ref, ob_ref, st_ref, *, c, nheads, hd):
    @pl.when(pl.program_id(1) == 0)
    def _():
        st_ref[...] = jnp.zeros_like(st_ref)

    _hgrn_dir(qf_ref[0], if_ref[0], ff_ref[0], lbf_ref[...], masks_ref, 0, st_ref, of_ref, False, c, nheads, hd)
    _hgrn_dir(qb_ref[0], ib_ref[0], fb_ref[0], lbb_ref[...], masks_ref, 1, st_ref, ob_ref, True, c, nheads, hd)


def _scan_chunk_maps(n_lat, n_ctx):
    n = n_lat + n_ctx
    fwd = lambda j: (j + n_lat) % n
    bwd = lambda j: n - 1 - j
    return n, fwd, bwd


def _hgrn_scan(u, cq, ci, cff, cfb, lbp_f, lbp_b, *, t_lat, c=CHUNK, nheads=A_HEADS, hd=A_DIM):
    b, r, _ = u.shape
    width = nheads * hd
    n, fwd, bwd = _scan_chunk_maps(t_lat // c, (r - t_lat) // c)
    masks = jnp.asarray(_level_masks(c))

    def spec(col, cm):
        return pl.BlockSpec((1, c, width), lambda bb, j: (bb, cm(j), col // width))

    const2 = pl.BlockSpec((3, width), lambda bb, j: (0, 0))
    out_shape = jax.ShapeDtypeStruct((b, r, width), F32)
    return pl.pallas_call(
        functools.partial(_hgrn_kernel, c=c, nheads=nheads, hd=hd),
        grid=(b, n),
        in_specs=[spec(cq, fwd), spec(ci, fwd), spec(cff, fwd), spec(cq, bwd), spec(ci, bwd), spec(cfb, bwd),
                  const2, const2,
                  pl.BlockSpec(masks.shape, lambda bb, j: (0, 0, 0, 0))],
        out_specs=[pl.BlockSpec((1, c, width), lambda bb, j: (bb, fwd(j), 0)),
                   pl.BlockSpec((1, c, width), lambda bb, j: (bb, bwd(j), 0))],
        out_shape=[out_shape, out_shape],
        scratch_shapes=[pltpu.VMEM((2, nheads, hd, hd), F32)],
        compiler_params=_cp(("arbitrary", "arbitrary")),
        name="hgrn2_scan",
    )(u, u, u, u, u, u, lbp_f, lbp_b, masks)


def _ssd_dir(xm, bm, cm, dtc_raw, dtr_raw, prow, pcol, tri_ref, d, st_ref, o_ref, rev, c):
    nh = M_HEADS
    dt_c = _softplus(dtc_raw + prow[0:1, :])
    lf_c = dt_c * prow[1:2, :]
    dt_r = _softplus(dtr_raw + pcol[:, 0:1])[d * nh:(d + 1) * nh]
    lf_r = dt_r * pcol[d * nh:(d + 1) * nh, 1:2]
    lo, up = tri_ref[0], tri_ref[1]
    if rev:
        b_c = _dot_exact(up, lf_c)
        b_r = _dot_exact(lf_r, lo)
        causal = up > 0.5
        btot = b_c[0:1, :]
    else:
        b_c = _dot_exact(lo, lf_c)
        b_r = _dot_exact(lf_r, up)
        causal = lo > 0.5
        btot = b_c[c - 1:c, :]
    lane = lax.broadcasted_iota(jnp.int32, (1, LANE), 1)
    left = lane < M_HEADDIM
    bd_mask = jnp.concatenate([jnp.broadcast_to(left, (M_STATE, LANE)),
                               jnp.broadcast_to(jnp.logical_not(left), (M_STATE, LANE))], axis=0)
    for g in range(M_GROUPS):
        cg = cm[:, g * M_STATE:(g + 1) * M_STATE]
        bg_t = bm[:, g * M_STATE:(g + 1) * M_STATE].T
        gmat = _dot(cg.astype(BF16), bg_t.astype(BF16))
        for pp in range(nh // (2 * M_GROUPS)):
            pair = g * (nh // (2 * M_GROUPS)) + pp
            lhs, qs, kts, decs = [], [], [], []
            for h in (2 * pair, 2 * pair + 1):
                hc = d * nh + h
                bcol = b_c[:, hc:hc + 1]
                brow = b_r[h:h + 1, :]
                dtrow = dt_r[h:h + 1, :]
                dec = jnp.where(causal, jnp.exp(bcol - brow), 0.0) * dtrow
                lhs.append(gmat * dec)
                qs.append(cg * jnp.exp(bcol))
                bt = btot[:, hc:hc + 1]
                kts.append(bg_t * (jnp.exp(bt - brow) * dtrow))
                decs.append(jnp.broadcast_to(jnp.exp(bt), (M_STATE, LANE)))
            xp = xm[:, pair * LANE:(pair + 1) * LANE]
            rhs = jnp.concatenate([jnp.where(left, xp, 0.0), jnp.where(left, 0.0, xp)], axis=0)
            st = st_ref[d, pair]
            o = _dot(jnp.concatenate(lhs, axis=1).astype(BF16), rhs.astype(BF16))
            o = o + _dot(jnp.concatenate(qs, axis=1).astype(BF16), st.astype(BF16))
            o_ref[0, :, pair * LANE:(pair + 1) * LANE] = o
            upd = _dot(jnp.concatenate(kts, axis=0).astype(BF16), xp.astype(BF16))
            st_ref[d, pair] = st * jnp.concatenate(decs, axis=0) + jnp.where(bd_mask, upd, 0.0)


def _ssd_kernel(xf_ref, bf_ref, cf_ref, dcf_ref, drf_ref, xb_ref, bb_ref, cb_ref, dcb_ref, drb_ref,
                prow_ref, pcol_ref, tri_ref, of_ref, ob_ref, st_ref, *, c):
    @pl.when(pl.program_id(1) == 0)
    def _():
        st_ref[...] = jnp.zeros_like(st_ref)

    _ssd_dir(xf_ref[0], bf_ref[0], cf_ref[0], dcf_ref[0], drf_ref[0], prow_ref[...], pcol_ref[...],
             tri_ref, 0, st_ref, of_ref, False, c)
    _ssd_dir(xb_ref[0], bb_ref[0], cb_ref[0], dcb_ref[0], drb_ref[0], prow_ref[...], pcol_ref[...],
             tri_ref, 1, st_ref, ob_ref, True, c)


def _tri_mats(c):
    t = np.arange(c)[:, None]
    s = np.arange(c)[None, :]
    return np.stack([(s <= t), (s >= t)]).astype(np.float32)


def _ssd_scan(xbc, u, dt_t, prow, pcol, *, t_lat, c=CHUNK):
    b, r, _ = xbc.shape
    n, fwd, bwd = _scan_chunk_maps(t_lat // c, (r - t_lat) // c)
    gs = M_GROUPS * M_STATE
    tri = jnp.asarray(_tri_mats(c))

    def specs(cm):
        return [pl.BlockSpec((1, c, M_INNER), lambda bb, j: (bb, cm(j), 0)),
                pl.BlockSpec((1, c, gs), lambda bb, j: (bb, cm(j), M_INNER // gs)),
                pl.BlockSpec((1, c, gs), lambda bb, j: (bb, cm(j), M_INNER // gs + 1)),
                pl.BlockSpec((1, c, LANE), lambda bb, j: (bb, cm(j), U_DT // LANE)),
                pl.BlockSpec((1, 2 * M_HEADS, c), lambda bb, j: (bb, 0, cm(j)))]

    out_shape = jax.ShapeDtypeStruct((b, r, M_INNER), F32)
    return pl.pallas_call(
        functools.partial(_ssd_kernel, c=c),
        grid=(b, n),
        in_specs=specs(fwd) + specs(bwd) + [
            pl.BlockSpec(prow.shape, lambda bb, j: (0, 0)),
            pl.BlockSpec(pcol.shape, lambda bb, j: (0, 0)),
            pl.BlockSpec(tri.shape, lambda bb, j: (0, 0, 0))],
        out_specs=[pl.BlockSpec((1, c, M_INNER), lambda bb, j: (bb, fwd(j), 0)),
                   pl.BlockSpec((1, c, M_INNER), lambda bb, j: (bb, bwd(j), 0))],
        out_shape=[out_shape, out_shape],
        scratch_shapes=[pltpu.VMEM((2, M_HEADS // 2, 2 * M_STATE, LANE), F32)],
        compiler_params=_cp(("arbitrary", "arbitrary")),
        name="ssd_scan",
    )(xbc, xbc, xbc, u, dt_t, xbc, xbc, xbc, u, dt_t, prow, pcol, tri)


def _band_mats(windows, tb, period):
    t = np.arange(tb)[:, None]
    s = np.arange(tb)[None, :]
    out = []
    for w in windows:
        left = w // 2
        right = w - 1 - left
        out.append((t // period == s // period) & (s >= t - left) & (s <= t + right))
    return np.stack(out).astype(np.float32)


def _clip_count(idx, n, left, right):
    return jnp.minimum(idx + right + 1, n) - jnp.maximum(idx - left, 0)


def _pool_kernel(z_ref, band_ref, gw_ref, sc_ref, *rest, n, tb, grid_w, windows):
    o_ref, pad_ref = rest[-2:]
    g = pl.program_id(1)
    band = band_ref[0].astype(BF16)
    shift = int(math.log2(grid_w)) if grid_w else 0
    for gi, w in enumerate(windows):
        @pl.when(g == gi)
        def _(w=w):
            left = w // 2
            right = w - 1 - left
            wr = w if grid_w else 1
            top = left * grid_w
            if grid_w:
                pad_ref[0:top, :] = jnp.zeros((top, LANE), F32)
                if right:
                    pad_ref[top + n:top + n + right * grid_w, :] = jnp.zeros((right * grid_w, LANE), F32)

            def col_sum(i, carry):
                zt = z_ref[0, pl.ds(i * tb, tb), :]
                hi, lo = _split_bf16(zt)
                pad_ref[pl.ds(top + i * tb, tb), :] = _dot(band, hi) + _dot(band, lo)
                return carry

            lax.fori_loop(0, n // tb, col_sum, 0)

            def finish(i, carry):
                acc = pad_ref[pl.ds(i * tb, tb), :]
                for kk in range(1, wr):
                    acc = acc + pad_ref[pl.ds(i * tb + kk * grid_w, tb), :]
                t = i * tb + lax.broadcasted_iota(jnp.int32, (tb, 1), 0)
                if grid_w:
                    cnt = (_clip_count(lax.shift_right_logical(t, shift), n // grid_w, left, right)
                           * _clip_count(t & (grid_w - 1), grid_w, left, right))
                else:
                    cnt = _clip_count(t, n, left, right)
                y = acc / cnt.astype(F32) - z_ref[0, pl.ds(i * tb, tb), :]
                o_ref[0, pl.ds(i * tb, tb), :] = _dot(y.astype(BF16), gw_ref[0].astype(BF16)) * sc_ref[0]
                return carry

            lax.fori_loop(0, n // tb, finish, 0)


def _pool(u, dst, group_w, scale, *, row0, n, grid_w):
    b, r, _ = u.shape
    tb = 2 * grid_w if grid_w else n
    bands = jnp.asarray(_band_mats(P_WINDOWS, tb, grid_w if grid_w else n))
    rb = row0 // n
    pad_rows = n + (max(P_WINDOWS) - 1) * grid_w
    return pl.pallas_call(
        functools.partial(_pool_kernel, n=n, tb=tb, grid_w=grid_w, windows=P_WINDOWS),
        grid=(b, len(P_WINDOWS)),
        in_specs=[
            pl.BlockSpec((1, n, P_GROUP), lambda bb, g: (bb, rb, U_P // P_GROUP + g)),
            pl.BlockSpec((1, tb, tb), lambda bb, g: (g, 0, 0)),
            pl.BlockSpec((1, P_GROUP, P_GROUP), lambda bb, g: (g, 0, 0)),
            pl.BlockSpec((1, 1, P_GROUP), lambda bb, g: (g, 0, 0)),
        ] + ([] if dst is None else [pl.BlockSpec(memory_space=pl.ANY)]),
        out_specs=pl.BlockSpec((1, n, P_GROUP), lambda bb, g: (bb, rb, g)),
        out_shape=jax.ShapeDtypeStruct((b, r, P_WIDTH), F32),
        scratch_shapes=[pltpu.VMEM((pad_rows, LANE), F32)],
        input_output_aliases={} if dst is None else {4: 0},
        compiler_params=_cp(("arbitrary", "arbitrary")),
        name="pool_grid" if grid_w else "pool_seq",
    )(u, bands, group_w, scale.reshape(len(P_WINDOWS), 1, P_GROUP), *(() if dst is None else (dst,)))


def _chan_dft_mats():
    k = np.arange(F_WIDTH // 4)
    ang = 2.0 * np.pi * ((k[:, None] * k[None, :]) % len(k)) / len(k)
    eye = np.eye(4)
    w = np.concatenate([np.kron(eye, np.cos(ang)), np.kron(eye, np.sin(ang))], axis=1)
    hi = w.astype(np.float32).astype(BF16)
    lo = (w - hi.astype(np.float64)).astype(np.float32).astype(BF16)
    return jnp.asarray(hi), jnp.asarray(lo)


def _chan_dft_kernel(z_ref, wh_ref, wl_ref, o_ref):
    zh, zl = _split_bf16(z_ref[0])
    acc = _dot(zh, wh_ref[...]) + _dot(zh, wl_ref[...]) + _dot(zl, wh_ref[...])
    o_ref[0] = acc.astype(o_ref.dtype)


def _chan_dft(u, *, tm):
    b, r, _ = u.shape
    wh, wl = _chan_dft_mats()
    wspec = pl.BlockSpec(wh.shape, lambda bb, i: (0, 0))
    return pl.pallas_call(
        _chan_dft_kernel,
        grid=(b, r // tm),
        in_specs=[pl.BlockSpec((1, tm, F_WIDTH), lambda bb, i: (bb, i, U_F // F_WIDTH)), wspec, wspec],
        out_specs=pl.BlockSpec((1, tm, 2 * F_WIDTH), lambda bb, i: (bb, i, 0)),
        out_shape=jax.ShapeDtypeStruct((b, r, 2 * F_WIDTH), BF16),
        compiler_params=_cp(("arbitrary", "arbitrary")),
        name="chan_dft",
    )(u, wh, wl)


def _dft_mats(n):
    k = jnp.arange(n, dtype=jnp.int32)
    ang = ((k[:, None] * k[None, :]) & (n - 1)).astype(F32) * (2.0 * math.pi / n)
    return jnp.cos(ang).astype(BF16), jnp.sin(ang).astype(BF16)


def _time_dft_kernel(ct_ref, st_ref, zc_ref, zs_ref, *rest, nb, scale):
    o_ref = rest[-1]
    kk = pl.program_id(1)

    @pl.when(kk == 0)
    def _():
        o_ref[...] = jnp.zeros_like(o_ref)

    for bb in range(nb):
        o_ref[bb] += _dot(ct_ref[...], zc_ref[bb]) - _dot(st_ref[...], zs_ref[bb])

    @pl.when(kk == pl.num_programs(1) - 1)
    def _():
        o_ref[...] = o_ref[...] * scale


def _time_dft(zcs, dst, ct, st, *, row0, n, tm, tk):
    b, r, _ = zcs.shape
    scale = 1.0 / math.sqrt(n * (F_WIDTH // 4))
    return pl.pallas_call(
        functools.partial(_time_dft_kernel, nb=b, scale=scale),
        grid=(n // tm, n // tk),
        in_specs=[
            pl.BlockSpec((tm, tk), lambda i, k: (i, k)),
            pl.BlockSpec((tm, tk), lambda i, k: (i, k)),
            pl.BlockSpec((b, tk, F_WIDTH), lambda i, k: (0, row0 // tk + k, 0)),
            pl.BlockSpec((b, tk, F_WIDTH), lambda i, k: (0, row0 // tk + k, 1)),
        ] + ([] if dst is None else [pl.BlockSpec(memory_space=pl.ANY)]),
        out_specs=pl.BlockSpec((b, tm, F_WIDTH), lambda i, k: (0, row0 // tm + i, 0)),
        out_shape=jax.ShapeDtypeStruct((b, r, F_WIDTH), F32),
        input_output_aliases={} if dst is None else {4: 0},
        compiler_params=_cp(("arbitrary", "arbitrary")),
        name="time_dft_%d" % n,
    )(ct, st, zcs, zcs, *(() if dst is None else (dst,)))


def _merge_kernel(oaf_ref, oab_ref, g_ref, omf_ref, omb_ref, xm_ref, z0_ref, z1_ref, pm_ref, fm_ref,
                  anw_ref, dsk_ref, mnw_ref, wbr_ref, g0_ref, g1_ref, g2_ref, g3_ref, o_ref, act_ref):
    @pl.when(pl.program_id(2) == 0)
    def _():
        oa = oaf_ref[0] + oab_ref[0]
        gate = _silu(g_ref[0])
        for hh in range(A_HEADS):
            sl = slice(hh * A_DIM, (hh + 1) * A_DIM)
            oh = oa[:, sl]
            ms = jnp.mean(oh * oh, axis=-1, keepdims=True)
            act_ref[:, sl] = (oh * lax.rsqrt(ms + EPS) * anw_ref[...] * gate[:, sl]).astype(BF16)
        y = omf_ref[0] + omb_ref[0] + dsk_ref[...] * xm_ref[0]
        z = jnp.concatenate([z0_ref[0], z1_ref[0]], axis=1)
        y = y * _silu(z)
        ms = jnp.mean(y * y, axis=-1, keepdims=True)
        act_ref[:, A_WIDTH:A_WIDTH + M_INNER] = (y * lax.rsqrt(ms + EPS) * mnw_ref[...]).astype(BF16)
        act_ref[:, A_WIDTH + M_INNER:A_WIDTH + M_INNER + P_WIDTH] = pm_ref[0].astype(BF16)
        act_ref[:, A_WIDTH + M_INNER + P_WIDTH:] = fm_ref[0].astype(BF16)

    offs = (0, A_WIDTH, A_WIDTH + M_INNER, A_WIDTH + M_INNER + P_WIDTH, ACT_W)
    acc = None
    for br, gr in enumerate((g0_ref, g1_ref, g2_ref, g3_ref)):
        yb = _dot(act_ref[:, offs[br]:offs[br + 1]], wbr_ref[offs[br]:offs[br + 1], :])
        term = _sigmoid(gr[0]) * yb
        acc = term if acc is None else acc + term
    o_ref[0] = acc.astype(o_ref.dtype)


def _merge(u, oaf, oab, omf, omb, xbc, pm, fm, anw, dsk, mnw, wbr, *, tm, tn=512):
    b, r, _ = u.shape

    def rows(width, cb):
        return pl.BlockSpec((1, tm, width), lambda bb, i, j: (bb, i, cb))

    def vec(width):
        return pl.BlockSpec((1, width), lambda bb, i, j: (0, 0))

    def gate(br):
        return pl.BlockSpec((1, tm, tn), lambda bb, i, j: (bb, i, (U_GATE + br * D_MODEL) // tn + j))

    return pl.pallas_call(
        _merge_kernel,
        grid=(b, r // tm, D_MODEL // tn),
        in_specs=[rows(A_WIDTH, 0), rows(A_WIDTH, 0), rows(A_WIDTH, U_G // A_WIDTH),
                  rows(M_INNER, 0), rows(M_INNER, 0), rows(M_INNER, 0),
                  rows(512, U_Z // 512), rows(512, U_Z // 512 + 1),
                  rows(P_WIDTH, 0), rows(F_WIDTH, 0),
                  vec(A_DIM), vec(M_INNER), vec(M_INNER),
                  pl.BlockSpec((ACT_W, tn), lambda bb, i, j: (0, j)),
                  gate(0), gate(1), gate(2), gate(3)],
        out_specs=pl.BlockSpec((1, tm, tn), lambda bb, i, j: (bb, i, j)),
        out_shape=jax.ShapeDtypeStruct((b, r, D_MODEL), BF16),
        scratch_shapes=[pltpu.VMEM((tm, ACT_W), BF16)],
        compiler_params=_cp(("arbitrary", "arbitrary", "arbitrary")),
        name="merge",
    )(oaf, oab, u, omf, omb, xbc, u, u, pm, fm, anw, dsk, mnw, wbr, u, u, u, u)


def _proj_resid_kernel(y_ref, w_ref, x_ref, gl_ref, gc_ref, o_ref, *, tm, t_lat):
    is_ctx = _row_is_ctx(pl.program_id(1) * tm, tm, t_lat)
    gate = jnp.where(is_ctx, gc_ref[...], gl_ref[0])
    o_ref[0] = x_ref[0] + gate * _dot(y_ref[0], w_ref[...])


def _proj_resid(y, w, x, gl, gc, *, t_lat, tm):
    b, r, d = x.shape
    k = y.shape[2]
    return pl.pallas_call(
        functools.partial(_proj_resid_kernel, tm=tm, t_lat=t_lat),
        grid=(b, r // tm),
        in_specs=[pl.BlockSpec((1, tm, k), lambda bb, i: (bb, i, 0)),
                  pl.BlockSpec((k, d), lambda bb, i: (0, 0)),
                  pl.BlockSpec((1, tm, d), lambda bb, i: (bb, i, 0)),
                  pl.BlockSpec((1, 1, d), lambda bb, i: (bb, 0, 0)),
                  pl.BlockSpec((1, d), lambda bb, i: (0, 0))],
        out_specs=pl.BlockSpec((1, tm, d), lambda bb, i: (bb, i, 0)),
        out_shape=jax.ShapeDtypeStruct((b, r, d), F32),
        compiler_params=_cp(("arbitrary", "arbitrary")),
        name="proj_resid",
    )(y, w, x, gl, gc)


def _ffn_down_kernel(ap_ref, am_ref, an_ref, bp_ref, bm_ref, bn_ref, wa_ref, wb_ref, ba_ref, bb_ref,
                     wd_ref, x_ref, gl_ref, gc_ref, fnw_ref, o_ref, acc_ref, ext_ref,
                     *, tm, t_lat, r_tot, final_norm):
    i = pl.program_id(1)
    kk = pl.program_id(2)

    @pl.when(kk == 0)
    def _():
        acc_ref[...] = jnp.zeros_like(acc_ref)

    a = _dwconv_rows(ext_ref, ap_ref[0], am_ref[0], an_ref[0], wa_ref[...], FFN_CONV, i * tm, t_lat, r_tot)
    a = a + ba_ref[...]
    g = _dwconv_rows(ext_ref, bp_ref[0], bm_ref[0], bn_ref[0], wb_ref[...], FFN_CONV, i * tm, t_lat, r_tot)
    g = g + bb_ref[...]
    acc_ref[...] += _dot((_silu(a) * g).astype(BF16), wd_ref[...])

    @pl.when(kk == pl.num_programs(2) - 1)
    def _():
        is_ctx = _row_is_ctx(i * tm, tm, t_lat)
        gate = jnp.where(is_ctx, gc_ref[...], gl_ref[0])
        y = x_ref[0] + gate * acc_ref[...]
        if final_norm:
            ms = jnp.mean(y * y, axis=-1, keepdims=True)
            y = y * lax.rsqrt(ms + EPS) * fnw_ref[...]
        o_ref[0] = y


def _ffn_down(v, cw, cb, wd, x, gl, gc, fnw, *, t_lat, tm, tk, final_norm, out_rows):
    b, r, d = x.shape
    nk = D_FF // tk
    a_prev, a_next = _halo_specs(tm, tk, lambda k: k, r)
    b_prev, b_next = _halo_specs(tm, tk, lambda k: nk + k, r)
    cb2 = cb.reshape(1, 2 * D_FF)
    return pl.pallas_call(
        functools.partial(_ffn_down_kernel, tm=tm, t_lat=t_lat, r_tot=r, final_norm=final_norm),
        grid=(b, r // tm, nk),
        in_specs=[a_prev, pl.BlockSpec((1, tm, tk), lambda bb, i, k: (bb, i, k)), a_next,
                  b_prev, pl.BlockSpec((1, tm, tk), lambda bb, i, k: (bb, i, nk + k)), b_next,
                  pl.BlockSpec((FFN_CONV, tk), lambda bb, i, k: (0, k)),
                  pl.BlockSpec((FFN_CONV, tk), lambda bb, i, k: (0, nk + k)),
                  pl.BlockSpec((1, tk), lambda bb, i, k: (0, k)),
                  pl.BlockSpec((1, tk), lambda bb, i, k: (0, nk + k)),
                  pl.BlockSpec((tk, d), lambda bb, i, k: (k, 0)),
                  pl.BlockSpec((1, tm, d), lambda bb, i, k: (bb, i, 0)),
                  pl.BlockSpec((1, 1, d), lambda bb, i, k: (bb, 0, 0)),
                  pl.BlockSpec((1, d), lambda bb, i, k: (0, 0)),
                  pl.BlockSpec((1, d), lambda bb, i, k: (0, 0))],
        out_specs=pl.BlockSpec((1, tm, d), lambda bb, i, k: (bb, i, 0)),
        out_shape=jax.ShapeDtypeStruct((b, out_rows, d), F32),
        scratch_shapes=[pltpu.VMEM((tm, d), F32), pltpu.VMEM((tm + 2 * SUBLANE, tk), F32)],
        compiler_params=_cp(("arbitrary", "arbitrary", "arbitrary")),
        name="ffn_down",
    )(v, v, v, v, v, v, cw, cw, cb2, cb2, wd, x, gl, gc, fnw)


def _pad_w_in(w_in_l):
    pad = jnp.zeros((D_MODEL, U_P - AM_COLS), w_in_l.dtype)
    return jnp.concatenate([w_in_l[:, :AM_COLS], pad, w_in_l[:, AM_COLS:]], axis=1).astype(BF16)


def kernel(x, c, ctx, c_ctx, w_ada, b_ada, norm1_w, w_in, a_lb_logits, a_norm_w, w_br_a, m_conv_w, m_conv_b,
           m_dt_bias, m_a_log, m_d, m_norm_w, w_br_m, p_group_w, p_scale, w_br_p, w_br_f, w_out, norm2_w,
           w_up, ffn_conv_w, ffn_conv_b, w_down, final_norm_w):
    nb, t_lat, d = x.shape
    t_ctx = ctx.shape[1]
    depth = w_ada.shape[0]
    tm_big, tm_small = 768, 384

    lb_all = jnp.cumsum(jax.nn.softmax(a_lb_logits.astype(F32), axis=0), axis=0)
    lb_all = lb_all - lb_all[0]

    xs = jnp.concatenate([x, ctx], axis=1)
    c_all = jnp.concatenate([c, c_ctx[None, :], jnp.zeros((SUBLANE - nb - 1, d), F32)], axis=0)
    mod = _ada(c_all, w_ada, b_ada)
    ct_lat, st_lat = _dft_mats(t_lat)
    ct_ctx, st_ctx = _dft_mats(t_ctx)

    for l in range(depth):
        last = l == depth - 1
        ml = mod[l, :nb].reshape(nb, 1, 6, d)
        mc = mod[l, nb].reshape(6, d)
        lat = lambda k: ml[:, :, k, :]
        cx = lambda k: mc[k:k + 1, :]

        u = _norm_mod_matmul(xs, norm1_w[l][None, :], lat(0), lat(1), cx(0), cx(1), _pad_w_in(w_in[l]),
                             t_lat=t_lat, tm=tm_big, tn=512)
        lbp = [jnp.stack([jnp.log(lb_all[l, dd]), jnp.log1p(-lb_all[l, dd]), 1.0 - lb_all[l, dd]])
               for dd in range(2)]
        oaf, oab = _hgrn_scan(u, U_Q, U_I, U_FF, U_FB, lbp[0], lbp[1], t_lat=t_lat)

        xbc = _conv_silu(u, U_XBC, M_XBC, m_conv_w[l], m_conv_b[l], t_lat=t_lat, tm=tm_big)
        dt_t = jnp.transpose(u[:, :, U_DT:U_DT + 2 * M_HEADS], (0, 2, 1))
        a_neg = -jnp.exp(m_a_log[l].astype(F32)).reshape(-1)
        dtb = m_dt_bias[l].astype(F32).reshape(-1)
        prow = jnp.zeros((2, LANE), F32).at[0, :2 * M_HEADS].set(dtb).at[1, :2 * M_HEADS].set(a_neg)
        pcol = jnp.stack([dtb, a_neg], axis=1)
        omf, omb = _ssd_scan(xbc, u, dt_t, prow, pcol, t_lat=t_lat)

        pm = _pool(u, None, p_group_w[l], p_scale[l], row0=0, n=t_lat, grid_w=GRID_W)
        pm = _pool(u, pm, p_group_w[l], p_scale[l], row0=t_lat, n=t_ctx, grid_w=0)
        zcs = _chan_dft(u, tm=tm_big)
        fm = _time_dft(zcs, None, ct_lat, st_lat, row0=0, n=t_lat, tm=1024, tk=1024)
        fm = _time_dft(zcs, fm, ct_ctx, st_ctx, row0=t_lat, n=t_ctx, tm=t_ctx, tk=t_ctx)

        wbr = jnp.concatenate([w_br_a[l], w_br_m[l], w_br_p[l], w_br_f[l]], axis=0).astype(BF16)
        dsk = jnp.repeat(m_d[l].astype(F32), M_HEADDIM)[None, :]
        anw = a_norm_w[l].astype(F32)[None, :]
        y = _merge(u, oaf, oab, omf, omb, xbc, pm, fm, anw, dsk, m_norm_w[l][None, :], wbr, tm=tm_small)
        xs = _proj_resid(y, w_out[l].astype(BF16), xs, lat(2), cx(2), t_lat=t_lat, tm=tm_small)

        v = _norm_mod_matmul(xs, norm2_w[l][None, :], lat(3), lat(4), cx(3), cx(4), w_up[l].astype(BF16),
                             t_lat=t_lat, tm=tm_big, tn=1024)
        xs = _ffn_down(v, ffn_conv_w[l], ffn_conv_b[l], w_down[l].astype(BF16), xs, lat(5), cx(5),
                       final_norm_w[None, :], t_lat=t_lat, tm=tm_small, tk=512, final_norm=last,
                       out_rows=t_lat if last else t_lat + t_ctx)
    return xs
```

```python
import functools
import math

import numpy as np
import jax
import jax.numpy as jnp
from jax import lax
from jax.experimental import pallas as pl
from jax.experimental.pallas import tpu as pltpu

F32 = jnp.float32
BF16 = jnp.bfloat16

D_MODEL = 2048
GRID_W = 64
EPS = 1e-6

A_HEADS = 4
A_DIM = 128
A_WIDTH = A_HEADS * A_DIM
M_HEADS = 16
M_HEADDIM = 64
M_INNER = M_HEADS * M_HEADDIM
M_GROUPS = 2
M_STATE = 128
M_XBC = M_INNER + 2 * M_GROUPS * M_STATE
P_WINDOWS = (2, 4, 8, 16)
P_GROUP = 128
P_WIDTH = 512
F_WIDTH = 512
N_BRANCH = 4
D_FF = 5632
FFN_CONV = 3

W_A, W_Z, W_XBC, W_DT, W_P, W_F, W_GATE = 0, 2560, 3584, 5120, 5152, 5664, 6176
W_END = W_GATE + N_BRANCH * D_MODEL

U_GATE = 0
U_Q, U_I, U_FF, U_FB, U_G = 8192, 8704, 9216, 9728, 10240
U_Z = 10752
U_XBC = 11776
U_P = 13312
U_F = 13824
U_DT = 14336
NU = 14592
ACT_W = A_WIDTH + M_INNER + P_WIDTH + F_WIDTH

LANE = 128
SUBLANE = 8
PACK = 16
CHUNK = 128
VMEM_LIMIT = 56 * 1024 * 1024


def _cp(sem):
    return pltpu.CompilerParams(dimension_semantics=sem, vmem_limit_bytes=VMEM_LIMIT)


def _sigmoid(x):
    return 1.0 / (1.0 + jnp.exp(-x))


def _silu(x):
    return x * _sigmoid(x)


def _softplus(x):
    return jnp.maximum(x, 0.0) + jnp.log(1.0 + jnp.exp(-jnp.abs(x)))


def _dot(a, b):
    return jnp.dot(a, b, preferred_element_type=F32)


def _dot_nt(a, b):
    return lax.dot_general(a, b, (((1,), (1,)), ((), ())), preferred_element_type=F32)


def _dot_exact(a, b):
    return jnp.dot(a, b, preferred_element_type=F32, precision=lax.Precision.HIGHEST)


def _row_is_ctx(row0, tm, t_lat):
    row = row0 + lax.broadcasted_iota(jnp.int32, (tm, 1), 0)
    return row >= t_lat


def _ada_kernel(c_ref, w_ref, b_ref, o_ref):
    s = _silu(c_ref[...])
    o_ref[0] = _dot(s.astype(BF16), w_ref[0].astype(BF16)) + b_ref[0]


def _ada(c_all, w_ada, b_ada, tn=1024):
    nl, d, n6 = w_ada.shape
    return pl.pallas_call(
        _ada_kernel,
        grid=(nl, n6 // tn),
        in_specs=[
            pl.BlockSpec((SUBLANE, d), lambda l, j: (0, 0)),
            pl.BlockSpec((1, d, tn), lambda l, j: (l, 0, j)),
            pl.BlockSpec((1, 1, tn), lambda l, j: (l, 0, j)),
        ],
        out_specs=pl.BlockSpec((1, SUBLANE, tn), lambda l, j: (l, 0, j)),
        out_shape=jax.ShapeDtypeStruct((nl, SUBLANE, n6), F32),
        compiler_params=_cp(("arbitrary", "arbitrary")),
        name="ada",
    )(c_all, w_ada, b_ada.reshape(nl, 1, n6))


def _norm_mod(x, is_ctx, nw, shl, scl, shc, scc):
    ms = jnp.mean(x * x, axis=-1, keepdims=True)
    y = x * lax.rsqrt(ms + EPS) * nw
    return y * (1.0 + jnp.where(is_ctx, scc, scl)) + jnp.where(is_ctx, shc, shl)


def _norm_mod_tile(x_ref, h_ref, tm, row0, t_lat, nw, shl, scl, shc, scc):
    def body(r, carry):
        rows = pl.ds(pl.multiple_of(r * PACK, PACK), PACK)
        is_ctx = _row_is_ctx(row0 + r * PACK, PACK, t_lat)
        h_ref[rows, :] = _norm_mod(x_ref[0, rows, :], is_ctx, nw, shl, scl, shc, scc).astype(BF16)
        return carry

    lax.fori_loop(0, tm // PACK, body, 0)


def _nmm_kernel(x_ref, nw_ref, shl_ref, scl_ref, shc_ref, scc_ref, w_ref, o_ref, h_ref, *, tm, t_lat):
    @pl.when(pl.program_id(2) == 0)
    def _():
        _norm_mod_tile(x_ref, h_ref, tm, pl.program_id(1) * tm, t_lat,
                       nw_ref[...], shl_ref[0], scl_ref[0], shc_ref[...], scc_ref[...])

    o_ref[0] = _dot(h_ref[...], w_ref[...]).astype(o_ref.dtype)


def _norm_mod_matmul(x, nw, shl, scl, shc, scc, w, *, t_lat, tm, tn):
    b, r, d = x.shape
    n = w.shape[1]
    vec_l = pl.BlockSpec((1, 1, d), lambda bb, i, j: (bb, 0, 0))
    vec_c = pl.BlockSpec((1, d), lambda bb, i, j: (0, 0))
    return pl.pallas_call(
        functools.partial(_nmm_kernel, tm=tm, t_lat=t_lat),
        grid=(b, r // tm, n // tn),
        in_specs=[
            pl.BlockSpec((1, tm, d), lambda bb, i, j: (bb, i, 0)),
            vec_c, vec_l, vec_l, vec_c, vec_c,
            pl.BlockSpec((d, tn), lambda bb, i, j: (0, j)),
        ],
        out_specs=pl.BlockSpec((1, tm, tn), lambda bb, i, j: (bb, i, j)),
        out_shape=jax.ShapeDtypeStruct((b, r, n), BF16),
        scratch_shapes=[pltpu.VMEM((tm, d), BF16)],
        compiler_params=_cp(("arbitrary", "arbitrary", "arbitrary")),
        name="norm_mod_matmul",
    )(x, nw, shl, scl, shc, scc, w)


def _halo_specs(tm, tc, col_of, r_tot, halo):
    nb = tm // halo
    last = r_tot // halo - 1
    prev = pl.BlockSpec((1, halo, tc), lambda b, i, j: (b, jnp.maximum(i * nb - 1, 0), col_of(j)))
    nxt = pl.BlockSpec((1, halo, tc), lambda b, i, j: (b, jnp.minimum((i + 1) * nb, last), col_of(j)))
    return prev, nxt


def _conv_silu_kernel(p_ref, m_ref, n_ref, w_ref, b_ref, o_ref, ext_ref, *, tm, ksize, t_lat, r_tot):
    main = m_ref[0].astype(F32)
    ext_ref[0:PACK, :] = p_ref[0].astype(F32)
    ext_ref[PACK:PACK + tm, :] = main
    ext_ref[PACK + tm:2 * PACK + tm, :] = n_ref[0].astype(F32)
    t = pl.program_id(1) * tm + lax.broadcasted_iota(jnp.int32, (tm, 1), 0)
    is_ctx = t >= t_lat
    lo = jnp.where(is_ctx, t_lat, 0)
    hi = jnp.where(is_ctx, r_tot, t_lat)
    w = w_ref[...]
    acc = main * w[ksize // 2:ksize // 2 + 1, :] + b_ref[...]
    for k in range(ksize):
        dk = k - ksize // 2
        if dk:
            n = t + dk
            sh = ext_ref[pl.ds(PACK + dk, tm), :]
            acc = acc + jnp.where((n >= lo) & (n < hi), sh, 0.0) * w[k:k + 1, :]
    o_ref[0] = _silu(acc).astype(o_ref.dtype)


def _conv_silu(u, col0, width, w, bias, *, t_lat, tm, tc=512):
    b, r, _ = u.shape
    ksize = w.shape[0]
    cb = col0 // tc
    prev, nxt = _halo_specs(tm, tc, lambda j: cb + j, r, PACK)
    return pl.pallas_call(
        functools.partial(_conv_silu_kernel, tm=tm, ksize=ksize, t_lat=t_lat, r_tot=r),
        grid=(b, r // tm, width // tc),
        in_specs=[
            prev,
            pl.BlockSpec((1, tm, tc), lambda bb, i, j: (bb, i, cb + j)),
            nxt,
            pl.BlockSpec((ksize, tc), lambda bb, i, j: (0, j)),
            pl.BlockSpec((1, tc), lambda bb, i, j: (0, j)),
        ],
        out_specs=pl.BlockSpec((1, tm, tc), lambda bb, i, j: (bb, i, j)),
        out_shape=jax.ShapeDtypeStruct((b, r, width), BF16),
        scratch_shapes=[pltpu.VMEM((tm + 2 * PACK, tc), F32)],
        compiler_params=_cp(("arbitrary", "arbitrary", "arbitrary")),
        name="conv_silu",
    )(u, u, u, w, bias.reshape(1, width))


def _level_masks(c):
    t = np.arange(c)[:, None]
    s = np.arange(c)[None, :]
    ms = [(t == s)]
    h = 1
    while h < c:
        g = 2 * h
        ms.append((t // g == s // g) & ((t % g) >= h) & ((s % g) < h))
        h *= 2
    fwd = np.stack(ms).astype(np.float32)
    return np.stack([fwd, np.transpose(fwd, (0, 2, 1))])


def _hgrn_dir(q_raw, v, f_raw, lbp, masks_ref, d, st_ref, o_ref, rev, c, nheads, hd):
    width = nheads * hd
    llb, l1m, oml = lbp[0:1, :], lbp[1:2, :], lbp[2:3, :]
    q = _silu(q_raw)
    e = jnp.exp(-jnp.abs(f_raw))
    inv = 1.0 / (1.0 + e)
    k = oml * jnp.where(f_raw >= 0, e * inv, inv)
    a = l1m + jnp.minimum(f_raw, 0.0) - jnp.log(1.0 + e)
    logf = jnp.maximum(llb, a) + jnp.log(1.0 + jnp.exp(-jnp.abs(llb - a)))

    t_idx = lax.broadcasted_iota(jnp.int32, (c, 1), 0)
    p = logf
    tot = logf
    levels = []
    h = 1
    while h < c:
        up = (t_idx & h) != 0
        tgt = jnp.logical_not(up) if rev else up
        levels.append(jnp.exp(jnp.where(tgt, p, tot - p)))
        if h < SUBLANE:
            sib = jnp.where(up, pltpu.roll(tot, h, 0), pltpu.roll(tot, c - h, 0))
        else:
            t4 = tot.reshape(c // (2 * h), 2, h, width)
            sib = jnp.concatenate([t4[:, 1:2], t4[:, 0:1]], axis=1).reshape(c, width)
        p = p + jnp.where(tgt, sib, 0.0)
        tot = tot + sib
        h *= 2
    eb = jnp.exp(p)
    ek = jnp.exp(tot - p)
    etot = jnp.exp(tot[0:1, :])

    for hh in range(nheads):
        sl = slice(hh * hd, (hh + 1) * hd)
        qh, kh, vh = q[:, sl], k[:, sl], v[:, sl]
        a_mat = _dot_nt(qh.astype(BF16), kh.astype(BF16)) * masks_ref[d, 0]
        for li, lev in enumerate(levels):
            eh = lev[:, sl]
            a_mat = a_mat + _dot_nt((qh * eh).astype(BF16), (kh * eh).astype(BF16)) * masks_ref[d, li + 1]
        st = st_ref[d, hh]
        o = _dot(a_mat.astype(BF16), vh.astype(BF16))
        o = o + _dot_nt((qh * eb[:, sl]).astype(BF16), st.astype(BF16))
        o_ref[0, :, sl] = o.astype(o_ref.dtype)
        khat = kh * ek[:, sl]
        st_ref[d, hh] = st * etot[:, sl] + _dot(vh.T.astype(BF16), khat.astype(BF16))


def _hgrn_kernel(qf_ref, if_ref, ff_ref, qb_ref, ib_ref, fb_ref, lbf_ref, lbb_ref, masks_ref,
                 of_ref, ob_ref, st_ref, *, c, nheads, hd):
    @pl.when(pl.program_id(1) == 0)
    def _():
        st_ref[...] = jnp.zeros_like(st_ref)

    ld = lambda ref: ref[0].astype(F32)
    _hgrn_dir(ld(qf_ref), ld(if_ref), ld(ff_ref), lbf_ref[...], masks_ref, 0, st_ref, of_ref, False, c, nheads, hd)
    _hgrn_dir(ld(qb_ref), ld(ib_ref), ld(fb_ref), lbb_ref[...], masks_ref, 1, st_ref, ob_ref, True, c, nheads, hd)


def _scan_chunk_maps(n_lat, n_ctx):
    n = n_lat + n_ctx
    fwd = lambda j: (j + n_lat) % n
    bwd = lambda j: n - 1 - j
    return n, fwd, bwd


def _hgrn_scan(u, cq, ci, cff, cfb, lbp_f, lbp_b, *, t_lat, c=CHUNK, nheads=A_HEADS, hd=A_DIM):
    b, r, _ = u.shape
    width = nheads * hd
    n, fwd, bwd = _scan_chunk_maps(t_lat // c, (r - t_lat) // c)
    masks = jnp.asarray(_level_masks(c))

    def spec(col, cm):
        return pl.BlockSpec((1, c, width), lambda bb, j: (bb, cm(j), col // width))

    const2 = pl.BlockSpec((3, width), lambda bb, j: (0, 0))
    out_shape = jax.ShapeDtypeStruct((b, r, width), BF16)
    return pl.pallas_call(
        functools.partial(_hgrn_kernel, c=c, nheads=nheads, hd=hd),
        grid=(b, n),
        in_specs=[spec(cq, fwd), spec(ci, fwd), spec(cff, fwd), spec(cq, bwd), spec(ci, bwd), spec(cfb, bwd),
                  const2, const2,
                  pl.BlockSpec(masks.shape, lambda bb, j: (0, 0, 0, 0))],
        out_specs=[pl.BlockSpec((1, c, width), lambda bb, j: (bb, fwd(j), 0)),
                   pl.BlockSpec((1, c, width), lambda bb, j: (bb, bwd(j), 0))],
        out_shape=[out_shape, out_shape],
        scratch_shapes=[pltpu.VMEM((2, nheads, hd, hd), F32)],
        compiler_params=_cp(("arbitrary", "arbitrary")),
        name="hgrn2_scan",
    )(u, u, u, u, u, u, lbp_f, lbp_b, masks)


def _ssd_dir(xm, bm, cm, dtc_raw, dtr_raw, prow, pcol, tri_ref, d, st_ref, o_ref, rev, c):
    nh = M_HEADS
    dt_c = _softplus(dtc_raw + prow[0:1, :])
    lf_c = dt_c * prow[1:2, :]
    dt_r = _softplus(dtr_raw + pcol[:, 0:1])[d * nh:(d + 1) * nh]
    lf_r = dt_r * pcol[d * nh:(d + 1) * nh, 1:2]
    lo, up = tri_ref[0], tri_ref[1]
    if rev:
        b_c = _dot_exact(up, lf_c)
        b_r = _dot_exact(lf_r, lo)
        causal = up > 0.5
        btot = b_c[0:1, :]
    else:
        b_c = _dot_exact(lo, lf_c)
        b_r = _dot_exact(lf_r, up)
        causal = lo > 0.5
        btot = b_c[c - 1:c, :]
    lane = lax.broadcasted_iota(jnp.int32, (1, LANE), 1)
    left = lane < M_HEADDIM
    bd_mask = jnp.concatenate([jnp.broadcast_to(left, (M_STATE, LANE)),
                               jnp.broadcast_to(jnp.logical_not(left), (M_STATE, LANE))], axis=0)
    for g in range(M_GROUPS):
        cg = cm[:, g * M_STATE:(g + 1) * M_STATE]
        bg_t = bm[:, g * M_STATE:(g + 1) * M_STATE].T
        gmat = _dot(cg.astype(BF16), bg_t.astype(BF16))
        for pp in range(nh // (2 * M_GROUPS)):
            pair = g * (nh // (2 * M_GROUPS)) + pp
            lhs, qs, kts, decs = [], [], [], []
            for h in (2 * pair, 2 * pair + 1):
                hc = d * nh + h
                bcol = b_c[:, hc:hc + 1]
                brow = b_r[h:h + 1, :]
                dtrow = dt_r[h:h + 1, :]
                dec = jnp.where(causal, jnp.exp(bcol - brow), 0.0) * dtrow
                lhs.append(gmat * dec)
                qs.append(cg * jnp.exp(bcol))
                bt = btot[:, hc:hc + 1]
                kts.append(bg_t * (jnp.exp(bt - brow) * dtrow))
                decs.append(jnp.broadcast_to(jnp.exp(bt), (M_STATE, LANE)))
            xp = xm[:, pair * LANE:(pair + 1) * LANE]
            rhs = jnp.concatenate([jnp.where(left, xp, 0.0), jnp.where(left, 0.0, xp)], axis=0)
            st = st_ref[d, pair]
            o = _dot(jnp.concatenate(lhs, axis=1).astype(BF16), rhs.astype(BF16))
            o = o + _dot(jnp.concatenate(qs, axis=1).astype(BF16), st.astype(BF16))
            o_ref[0, :, pair * LANE:(pair + 1) * LANE] = o.astype(o_ref.dtype)
            upd = _dot(jnp.concatenate(kts, axis=0).astype(BF16), xp.astype(BF16))
            st_ref[d, pair] = st * jnp.concatenate(decs, axis=0) + jnp.where(bd_mask, upd, 0.0)


def _ssd_kernel(xf_ref, bf_ref, cf_ref, dcf_ref, drf_ref, xb_ref, bb_ref, cb_ref, dcb_ref, drb_ref,
                prow_ref, pcol_ref, tri_ref, of_ref, ob_ref, st_ref, *, c):
    @pl.when(pl.program_id(1) == 0)
    def _():
        st_ref[...] = jnp.zeros_like(st_ref)

    ld = lambda ref: ref[0].astype(F32)
    _ssd_dir(ld(xf_ref), ld(bf_ref), ld(cf_ref), ld(dcf_ref), drf_ref[0], prow_ref[...], pcol_ref[...],
             tri_ref, 0, st_ref, of_ref, False, c)
    _ssd_dir(ld(xb_ref), ld(bb_ref), ld(cb_ref), ld(dcb_ref), drb_ref[0], prow_ref[...], pcol_ref[...],
             tri_ref, 1, st_ref, ob_ref, True, c)


def _tri_mats(c):
    t = np.arange(c)[:, None]
    s = np.arange(c)[None, :]
    return np.stack([(s <= t), (s >= t)]).astype(np.float32)


def _ssd_scan(xbc, u, dt_t, prow, pcol, *, t_lat, c=CHUNK):
    b, r, _ = xbc.shape
    n, fwd, bwd = _scan_chunk_maps(t_lat // c, (r - t_lat) // c)
    gs = M_GROUPS * M_STATE
    tri = jnp.asarray(_tri_mats(c))

    def specs(cm):
        return [pl.BlockSpec((1, c, M_INNER), lambda bb, j: (bb, cm(j), 0)),
                pl.BlockSpec((1, c, gs), lambda bb, j: (bb, cm(j), M_INNER // gs)),
                pl.BlockSpec((1, c, gs), lambda bb, j: (bb, cm(j), M_INNER // gs + 1)),
                pl.BlockSpec((1, c, LANE), lambda bb, j: (bb, cm(j), U_DT // LANE)),
                pl.BlockSpec((1, 2 * M_HEADS, c), lambda bb, j: (bb, 0, cm(j)))]

    out_shape = jax.ShapeDtypeStruct((b, r, M_INNER), BF16)
    return pl.pallas_call(
        functools.partial(_ssd_kernel, c=c),
        grid=(b, n),
        in_specs=specs(fwd) + specs(bwd) + [
            pl.BlockSpec(prow.shape, lambda bb, j: (0, 0)),
            pl.BlockSpec(pcol.shape, lambda bb, j: (0, 0)),
            pl.BlockSpec(tri.shape, lambda bb, j: (0, 0, 0))],
        out_specs=[pl.BlockSpec((1, c, M_INNER), lambda bb, j: (bb, fwd(j), 0)),
                   pl.BlockSpec((1, c, M_INNER), lambda bb, j: (bb, bwd(j), 0))],
        out_shape=[out_shape, out_shape],
        scratch_shapes=[pltpu.VMEM((2, M_HEADS // 2, 2 * M_STATE, LANE), F32)],
        compiler_params=_cp(("arbitrary", "arbitrary")),
        name="ssd_scan",
    )(xbc, xbc, xbc, u, dt_t, xbc, xbc, xbc, u, dt_t, prow, pcol, tri)


def _band_mats(windows, tb, period):
    t = np.arange(tb)[:, None]
    s = np.arange(tb)[None, :]
    out = []
    for w in windows:
        left = w // 2
        right = w - 1 - left
        out.append((t // period == s // period) & (s >= t - left) & (s <= t + right))
    return np.stack(out).astype(np.float32)


def _clip_count(idx, n, left, right):
    return jnp.minimum(idx + right + 1, n) - jnp.maximum(idx - left, 0)


def _pool_kernel(z_ref, band_ref, gw_ref, sc_ref, *rest, n, tb, grid_w, windows):
    o_ref, pad_ref = rest[-2:]
    g = pl.program_id(1)
    band = band_ref[0].astype(BF16)
    shift = int(math.log2(grid_w)) if grid_w else 0
    for gi, w in enumerate(windows):
        @pl.when(g == gi)
        def _(w=w):
            left = w // 2
            right = w - 1 - left
            wr = w if grid_w else 1
            top = left * grid_w
            if grid_w:
                pad_ref[0:top, :] = jnp.zeros((top, LANE), F32)
                if right:
                    pad_ref[top + n:top + n + right * grid_w, :] = jnp.zeros((right * grid_w, LANE), F32)

            def col_sum(i, carry):
                pad_ref[pl.ds(top + i * tb, tb), :] = _dot(band, z_ref[0, pl.ds(i * tb, tb), :].astype(BF16))
                return carry

            lax.fori_loop(0, n // tb, col_sum, 0)

            def finish(i, carry):
                acc = pad_ref[pl.ds(i * tb, tb), :]
                for kk in range(1, wr):
                    acc = acc + pad_ref[pl.ds(i * tb + kk * grid_w, tb), :]
                t = i * tb + lax.broadcasted_iota(jnp.int32, (tb, 1), 0)
                if grid_w:
                    cnt = (_clip_count(lax.shift_right_logical(t, shift), n // grid_w, left, right)
                           * _clip_count(t & (grid_w - 1), grid_w, left, right))
                else:
                    cnt = _clip_count(t, n, left, right)
                y = acc / cnt.astype(F32) - z_ref[0, pl.ds(i * tb, tb), :].astype(F32)
                out = _dot(y.astype(BF16), gw_ref[0].astype(BF16)) * sc_ref[0]
                o_ref[0, pl.ds(i * tb, tb), :] = out.astype(o_ref.dtype)
                return carry

            lax.fori_loop(0, n // tb, finish, 0)


def _pool(u, dst, group_w, scale, *, row0, n, grid_w):
    b, r, _ = u.shape
    tb = 2 * grid_w if grid_w else n
    bands = jnp.asarray(_band_mats(P_WINDOWS, tb, grid_w if grid_w else n))
    rb = row0 // n
    pad_rows = n + (max(P_WINDOWS) - 1) * grid_w
    return pl.pallas_call(
        functools.partial(_pool_kernel, n=n, tb=tb, grid_w=grid_w, windows=P_WINDOWS),
        grid=(b, len(P_WINDOWS)),
        in_specs=[
            pl.BlockSpec((1, n, P_GROUP), lambda bb, g: (bb, rb, U_P // P_GROUP + g)),
            pl.BlockSpec((1, tb, tb), lambda bb, g: (g, 0, 0)),
            pl.BlockSpec((1, P_GROUP, P_GROUP), lambda bb, g: (g, 0, 0)),
            pl.BlockSpec((1, 1, P_GROUP), lambda bb, g: (g, 0, 0)),
        ] + ([] if dst is None else [pl.BlockSpec(memory_space=pl.ANY)]),
        out_specs=pl.BlockSpec((1, n, P_GROUP), lambda bb, g: (bb, rb, g)),
        out_shape=jax.ShapeDtypeStruct((b, r, P_WIDTH), BF16),
        scratch_shapes=[pltpu.VMEM((pad_rows, LANE), F32)],
        input_output_aliases={} if dst is None else {4: 0},
        compiler_params=_cp(("arbitrary", "arbitrary")),
        name="pool_grid" if grid_w else "pool_seq",
    )(u, bands, group_w, scale.reshape(len(P_WINDOWS), 1, P_GROUP), *(() if dst is None else (dst,)))


def _chan_dft_mats():
    k = np.arange(F_WIDTH // 4)
    ang = 2.0 * np.pi * ((k[:, None] * k[None, :]) % len(k)) / len(k)
    eye = np.eye(4)
    w = np.concatenate([np.kron(eye, np.cos(ang)), np.kron(eye, np.sin(ang))], axis=1)
    hi = w.astype(np.float32).astype(BF16)
    lo = (w - hi.astype(np.float64)).astype(np.float32).astype(BF16)
    return jnp.asarray(hi), jnp.asarray(lo)


def _chan_dft_kernel(z_ref, wh_ref, wl_ref, o_ref):
    z = z_ref[0]
    o_ref[0] = (_dot(z, wh_ref[...]) + _dot(z, wl_ref[...])).astype(o_ref.dtype)


def _chan_dft(u, *, tm):
    b, r, _ = u.shape
    wh, wl = _chan_dft_mats()
    wspec = pl.BlockSpec(wh.shape, lambda bb, i: (0, 0))
    return pl.pallas_call(
        _chan_dft_kernel,
        grid=(b, r // tm),
        in_specs=[pl.BlockSpec((1, tm, F_WIDTH), lambda bb, i: (bb, i, U_F // F_WIDTH)), wspec, wspec],
        out_specs=pl.BlockSpec((1, tm, 2 * F_WIDTH), lambda bb, i: (bb, i, 0)),
        out_shape=jax.ShapeDtypeStruct((b, r, 2 * F_WIDTH), BF16),
        compiler_params=_cp(("arbitrary", "arbitrary")),
        name="chan_dft",
    )(u, wh, wl)


def _dft_cols(n, ncols):
    k = jnp.arange(n, dtype=jnp.int32)[:, None]
    j = jnp.arange(ncols, dtype=jnp.int32)[None, :]
    ang = ((k * j) & (n - 1)).astype(F32) * (2.0 * math.pi / n)
    return jnp.cos(ang), jnp.sin(ang)


def _time_dft_kernel(cb_ref, sb_ref, zc_ref, zs_ref, *rest, nb, n, tm, tk, scale):
    o_ref, acc_ref = rest[-2:]
    kk = pl.program_id(1)

    @pl.when(kk == 0)
    def _():
        acc_ref[...] = jnp.zeros_like(acc_ref)

    row = pl.program_id(0) * tm + lax.broadcasted_iota(jnp.int32, (tm, 1), 0)
    ang = ((row * (kk * tk)) & (n - 1)).astype(F32) * (2.0 * math.pi / n)
    ca, sa = jnp.cos(ang), jnp.sin(ang)
    cb, sb = cb_ref[...], sb_ref[...]
    ct = (ca * cb - sa * sb).astype(BF16)
    st = (sa * cb + ca * sb).astype(BF16)
    for bb in range(nb):
        acc_ref[bb] += _dot(ct, zc_ref[bb]) - _dot(st, zs_ref[bb])

    @pl.when(kk == pl.num_programs(1) - 1)
    def _():
        o_ref[...] = (acc_ref[...] * scale).astype(o_ref.dtype)


def _time_dft(zcs, dst, *, row0, n, tm, tk):
    b, r, _ = zcs.shape
    scale = 1.0 / math.sqrt(n * (F_WIDTH // 4))
    cb, sb = _dft_cols(n, tk)
    return pl.pallas_call(
        functools.partial(_time_dft_kernel, nb=b, n=n, tm=tm, tk=tk, scale=scale),
        grid=(n // tm, n // tk),
        in_specs=[
            pl.BlockSpec((tm, tk), lambda i, k: (i, 0)),
            pl.BlockSpec((tm, tk), lambda i, k: (i, 0)),
            pl.BlockSpec((b, tk, F_WIDTH), lambda i, k: (0, row0 // tk + k, 0)),
            pl.BlockSpec((b, tk, F_WIDTH), lambda i, k: (0, row0 // tk + k, 1)),
        ] + ([] if dst is None else [pl.BlockSpec(memory_space=pl.ANY)]),
        out_specs=pl.BlockSpec((b, tm, F_WIDTH), lambda i, k: (0, row0 // tm + i, 0)),
        out_shape=jax.ShapeDtypeStruct((b, r, F_WIDTH), BF16),
        scratch_shapes=[pltpu.VMEM((b, tm, F_WIDTH), F32)],
        input_output_aliases={} if dst is None else {4: 0},
        compiler_params=_cp(("arbitrary", "arbitrary")),
        name="time_dft_%d" % n,
    )(cb, sb, zcs, zcs, *(() if dst is None else (dst,)))


def _merge_kernel(oaf_ref, oab_ref, g_ref, omf_ref, omb_ref, xm_ref, z0_ref, z1_ref, pm_ref, fm_ref,
                  anw_ref, dsk_ref, mnw_ref, wbr_ref, g0_ref, g1_ref, g2_ref, g3_ref, o_ref, act_ref):
    @pl.when(pl.program_id(2) == 0)
    def _():
        ld = lambda ref: ref[0].astype(F32)
        oa = ld(oaf_ref) + ld(oab_ref)
        gate = _silu(ld(g_ref))
        for hh in range(A_HEADS):
            sl = slice(hh * A_DIM, (hh + 1) * A_DIM)
            oh = oa[:, sl]
            ms = jnp.mean(oh * oh, axis=-1, keepdims=True)
            act_ref[:, sl] = (oh * lax.rsqrt(ms + EPS) * anw_ref[...] * gate[:, sl]).astype(BF16)
        y = ld(omf_ref) + ld(omb_ref) + dsk_ref[...] * ld(xm_ref)
        z = jnp.concatenate([ld(z0_ref), ld(z1_ref)], axis=1)
        y = y * _silu(z)
        ms = jnp.mean(y * y, axis=-1, keepdims=True)
        act_ref[:, A_WIDTH:A_WIDTH + M_INNER] = (y * lax.rsqrt(ms + EPS) * mnw_ref[...]).astype(BF16)
        act_ref[:, A_WIDTH + M_INNER:A_WIDTH + M_INNER + P_WIDTH] = pm_ref[0].astype(BF16)
        act_ref[:, A_WIDTH + M_INNER + P_WIDTH:] = fm_ref[0].astype(BF16)

    offs = (0, A_WIDTH, A_WIDTH + M_INNER, A_WIDTH + M_INNER + P_WIDTH, ACT_W)
    acc = None
    for br, gr in enumerate((g0_ref, g1_ref, g2_ref, g3_ref)):
        yb = _dot(act_ref[:, offs[br]:offs[br + 1]], wbr_ref[offs[br]:offs[br + 1], :])
        term = _sigmoid(gr[0].astype(F32)) * yb
        acc = term if acc is None else acc + term
    o_ref[0] = acc.astype(o_ref.dtype)


def _merge(u, oaf, oab, omf, omb, xbc, pm, fm, anw, dsk, mnw, wbr, *, tm, tn=512):
    b, r, _ = u.shape

    def rows(width, cb):
        return pl.BlockSpec((1, tm, width), lambda bb, i, j: (bb, i, cb))

    def vec(width):
        return pl.BlockSpec((1, width), lambda bb, i, j: (0, 0))

    def gate(br):
        return pl.BlockSpec((1, tm, tn), lambda bb, i, j: (bb, i, (U_GATE + br * D_MODEL) // tn + j))

    return pl.pallas_call(
        _merge_kernel,
        grid=(b, r // tm, D_MODEL // tn),
        in_specs=[rows(A_WIDTH, 0), rows(A_WIDTH, 0), rows(A_WIDTH, U_G // A_WIDTH),
                  rows(M_INNER, 0), rows(M_INNER, 0), rows(M_INNER, 0),
                  rows(512, U_Z // 512), rows(512, U_Z // 512 + 1),
                  rows(P_WIDTH, 0), rows(F_WIDTH, 0),
                  vec(A_DIM), vec(M_INNER), vec(M_INNER),
                  pl.BlockSpec((ACT_W, tn), lambda bb, i, j: (0, j)),
                  gate(0), gate(1), gate(2), gate(3)],
        out_specs=pl.BlockSpec((1, tm, tn), lambda bb, i, j: (bb, i, j)),
        out_shape=jax.ShapeDtypeStruct((b, r, D_MODEL), BF16),
        scratch_shapes=[pltpu.VMEM((tm, ACT_W), BF16)],
        compiler_params=_cp(("arbitrary", "arbitrary", "arbitrary")),
        name="merge",
    )(oaf, oab, u, omf, omb, xbc, u, u, pm, fm, anw, dsk, mnw, wbr, u, u, u, u)


def _proj_resid_kernel(y_ref, w_ref, x_ref, gl_ref, gc_ref, o_ref, *, tm, t_lat):
    is_ctx = _row_is_ctx(pl.program_id(1) * tm, tm, t_lat)
    gate = jnp.where(is_ctx, gc_ref[...], gl_ref[0])
    o_ref[0] = x_ref[0] + gate * _dot(y_ref[0], w_ref[...])


def _proj_resid(y, w, x, gl, gc, *, t_lat, tm):
    b, r, d = x.shape
    k = y.shape[2]
    return pl.pallas_call(
        functools.partial(_proj_resid_kernel, tm=tm, t_lat=t_lat),
        grid=(b, r // tm),
        in_specs=[pl.BlockSpec((1, tm, k), lambda bb, i: (bb, i, 0)),
                  pl.BlockSpec((k, d), lambda bb, i: (0, 0)),
                  pl.BlockSpec((1, tm, d), lambda bb, i: (bb, i, 0)),
                  pl.BlockSpec((1, 1, d), lambda bb, i: (bb, 0, 0)),
                  pl.BlockSpec((1, d), lambda bb, i: (0, 0))],
        out_specs=pl.BlockSpec((1, tm, d), lambda bb, i: (bb, i, 0)),
        out_shape=jax.ShapeDtypeStruct((b, r, d), F32),
        compiler_params=_cp(("arbitrary", "arbitrary")),
        name="proj_resid",
    )(y, w, x, gl, gc)


def _conv3(ext, s_ref, w, bias, tm, m_prev, m_next):
    main = ext[0:tm]
    s_ref[SUBLANE:SUBLANE + tm, :] = main
    s_ref[SUBLANE - 1:SUBLANE, :] = ext[tm + SUBLANE - 1:tm + SUBLANE]
    s_ref[SUBLANE + tm:SUBLANE + tm + 1, :] = ext[tm + SUBLANE:tm + SUBLANE + 1]
    prev = s_ref[pl.ds(SUBLANE - 1, tm), :]
    nxt = s_ref[pl.ds(SUBLANE + 1, tm), :]
    return main * w[1:2, :] + (prev * m_prev) * w[0:1, :] + (nxt * m_next) * w[2:3, :] + bias


def _ffn_kernel(xp_ref, x_ref, xn_ref, nw_ref, shl_ref, scl_ref, shc_ref, scc_ref, gl_ref, gc_ref, fnw_ref,
                wa_ref, wb_ref, cwa_ref, cwb_ref, ba_ref, bb_ref, wd_ref, o_ref, h_ref, sa_ref, sb_ref,
                *, tm, t_lat, r_tot, final_norm):
    i = pl.program_id(1)
    kk = pl.program_id(2)
    row0 = i * tm

    @pl.when(kk == 0)
    def _():
        vecs = (nw_ref[...], shl_ref[0], scl_ref[0], shc_ref[...], scc_ref[...])
        _norm_mod_tile(x_ref, h_ref, tm, row0, t_lat, *vecs)
        halo = jnp.concatenate([xp_ref[0], xn_ref[0]], axis=0)
        hrow = lax.broadcasted_iota(jnp.int32, (2 * SUBLANE, 1), 0)
        grow = jnp.where(hrow < SUBLANE, row0 - SUBLANE + hrow, row0 + tm - SUBLANE + hrow)
        h_ref[tm:tm + 2 * SUBLANE, :] = _norm_mod(halo, grow >= t_lat, *vecs).astype(BF16)
        o_ref[...] = jnp.zeros_like(o_ref)

    t = row0 + lax.broadcasted_iota(jnp.int32, (tm, 1), 0)
    m_prev = jnp.where((t == 0) | (t == t_lat), 0.0, 1.0)
    m_next = jnp.where((t == t_lat - 1) | (t == r_tot - 1), 0.0, 1.0)
    h = h_ref[...]
    a = _conv3(_dot(h, wa_ref[...]), sa_ref, cwa_ref[...], ba_ref[...], tm, m_prev, m_next)
    g = _conv3(_dot(h, wb_ref[...]), sb_ref, cwb_ref[...], bb_ref[...], tm, m_prev, m_next)
    o_ref[0] += _dot((_silu(a) * g).astype(BF16), wd_ref[...])

    @pl.when(kk == pl.num_programs(2) - 1)
    def _():
        gate = jnp.where(_row_is_ctx(row0, tm, t_lat), gc_ref[...], gl_ref[0])
        y = x_ref[0] + gate * o_ref[0]
        if final_norm:
            ms = jnp.mean(y * y, axis=-1, keepdims=True)
            y = y * lax.rsqrt(ms + EPS) * fnw_ref[...]
        o_ref[0] = y


def _ffn(x, nw, shl, scl, shc, scc, gl, gc, fnw, w_up, cw, cb, wd, *, t_lat, tm, tk, final_norm, out_rows):
    b, r, d = x.shape
    nk = D_FF // tk
    x_prev, x_next = _halo_specs(tm, d, lambda k: 0, r, SUBLANE)
    vec_l = pl.BlockSpec((1, 1, d), lambda bb, i, k: (bb, 0, 0))
    vec_c = pl.BlockSpec((1, d), lambda bb, i, k: (0, 0))
    cb2 = cb.reshape(1, 2 * D_FF)
    return pl.pallas_call(
        functools.partial(_ffn_kernel, tm=tm, t_lat=t_lat, r_tot=r, final_norm=final_norm),
        grid=(b, r // tm, nk),
        in_specs=[x_prev, pl.BlockSpec((1, tm, d), lambda bb, i, k: (bb, i, 0)), x_next,
                  vec_c, vec_l, vec_l, vec_c, vec_c, vec_l, vec_c, vec_c,
                  pl.BlockSpec((d, tk), lambda bb, i, k: (0, k)),
                  pl.BlockSpec((d, tk), lambda bb, i, k: (0, nk + k)),
                  pl.BlockSpec((FFN_CONV, tk), lambda bb, i, k: (0, k)),
                  pl.BlockSpec((FFN_CONV, tk), lambda bb, i, k: (0, nk + k)),
                  pl.BlockSpec((1, tk), lambda bb, i, k: (0, k)),
                  pl.BlockSpec((1, tk), lambda bb, i, k: (0, nk + k)),
                  pl.BlockSpec((tk, d), lambda bb, i, k: (k, 0))],
        out_specs=pl.BlockSpec((1, tm, d), lambda bb, i, k: (bb, i, 0)),
        out_shape=jax.ShapeDtypeStruct((b, out_rows, d), F32),
        scratch_shapes=[pltpu.VMEM((tm + 2 * SUBLANE, d), BF16),
                        pltpu.VMEM((tm + 2 * SUBLANE, tk), F32),
                        pltpu.VMEM((tm + 2 * SUBLANE, tk), F32)],
        compiler_params=_cp(("arbitrary", "arbitrary", "arbitrary")),
        name="ffn",
    )(x, x, x, nw, shl, scl, shc, scc, gl, gc, fnw, w_up, w_up, cw, cw, cb2, cb2, wd)


def _arrange_w_in(w):
    pad = jnp.zeros((w.shape[0], NU - W_END), w.dtype)
    return jnp.concatenate([w[:, W_GATE:W_END], w[:, W_A:W_DT], w[:, W_P:W_GATE], w[:, W_DT:W_P], pad],
                           axis=1).astype(BF16)


def kernel(x, c, ctx, c_ctx, w_ada, b_ada, norm1_w, w_in, a_lb_logits, a_norm_w, w_br_a, m_conv_w, m_conv_b,
           m_dt_bias, m_a_log, m_d, m_norm_w, w_br_m, p_group_w, p_scale, w_br_p, w_br_f, w_out, norm2_w,
           w_up, ffn_conv_w, ffn_conv_b, w_down, final_norm_w):
    nb, t_lat, d = x.shape
    t_ctx = ctx.shape[1]
    depth = w_ada.shape[0]
    tm_in, tm = 1056, 768

    lb_all = jnp.cumsum(jax.nn.softmax(a_lb_logits.astype(F32), axis=0), axis=0)
    lb_all = lb_all - lb_all[0]

    xs = jnp.concatenate([x, ctx], axis=1)
    c_all = jnp.concatenate([c, c_ctx[None, :], jnp.zeros((SUBLANE - nb - 1, d), F32)], axis=0)
    mod = _ada(c_all, w_ada, b_ada)

    for l in range(depth):
        last = l == depth - 1
        ml = mod[l, :nb].reshape(nb, 1, 6, d)
        mc = mod[l, nb].reshape(6, d)
        lat = lambda k: ml[:, :, k, :]
        cx = lambda k: mc[k:k + 1, :]

        u = _norm_mod_matmul(xs, norm1_w[l][None, :], lat(0), lat(1), cx(0), cx(1), _arrange_w_in(w_in[l]),
                             t_lat=t_lat, tm=tm_in, tn=768)
        lbp = [jnp.stack([jnp.log(lb_all[l, dd]), jnp.log1p(-lb_all[l, dd]), 1.0 - lb_all[l, dd]])
               for dd in range(2)]
        oaf, oab = _hgrn_scan(u, U_Q, U_I, U_FF, U_FB, lbp[0], lbp[1], t_lat=t_lat)

        xbc = _conv_silu(u, U_XBC, M_XBC, m_conv_w[l], m_conv_b[l], t_lat=t_lat, tm=tm)
        dt_t = jnp.transpose(u[:, :, U_DT:U_DT + 2 * M_HEADS].astype(F32), (0, 2, 1))
        a_neg = -jnp.exp(m_a_log[l].astype(F32)).reshape(-1)
        dtb = m_dt_bias[l].astype(F32).reshape(-1)
        prow = jnp.zeros((2, LANE), F32).at[0, :2 * M_HEADS].set(dtb).at[1, :2 * M_HEADS].set(a_neg)
        pcol = jnp.stack([dtb, a_neg], axis=1)
        omf, omb = _ssd_scan(xbc, u, dt_t, prow, pcol, t_lat=t_lat)

        pm = _pool(u, None, p_group_w[l], p_scale[l], row0=0, n=t_lat, grid_w=GRID_W)
        pm = _pool(u, pm, p_group_w[l], p_scale[l], row0=t_lat, n=t_ctx, grid_w=0)
        zcs = _chan_dft(u, tm=tm)
        fm = _time_dft(zcs, None, row0=0, n=t_lat, tm=1024, tk=512)
        fm = _time_dft(zcs, fm, row0=t_lat, n=t_ctx, tm=t_ctx, tk=t_ctx)

        wbr = jnp.concatenate([w_br_a[l], w_br_m[l], w_br_p[l], w_br_f[l]], axis=0).astype(BF16)
        dsk = jnp.repeat(m_d[l].astype(F32), M_HEADDIM)[None, :]
        anw = a_norm_w[l].astype(F32)[None, :]
        y = _merge(u, oaf, oab, omf, omb, xbc, pm, fm, anw, dsk, m_norm_w[l][None, :], wbr, tm=tm)
        xs = _proj_resid(y, w_out[l].astype(BF16), xs, lat(2), cx(2), t_lat=t_lat, tm=tm)

        xs = _ffn(xs, norm2_w[l][None, :], lat(3), lat(4), cx(3), cx(4), lat(5), cx(5), final_norm_w[None, :],
                  w_up[l].astype(BF16), ffn_conv_w[l], ffn_conv_b[l], w_down[l].astype(BF16),
                  t_lat=t_lat, tm=tm, tk=512, final_norm=last, out_rows=t_lat if last else t_lat + t_ctx)
    return xs
```

```python
import functools
import math

import numpy as np
import jax
import jax.numpy as jnp
from jax import lax
from jax.experimental import pallas as pl
from jax.experimental.pallas import tpu as pltpu

F32 = jnp.float32
BF16 = jnp.bfloat16

D_MODEL = 2048
GRID_W = 64
EPS = 1e-6

A_HEADS = 4
A_DIM = 128
A_WIDTH = A_HEADS * A_DIM
M_HEADS = 16
M_HEADDIM = 64
M_INNER = M_HEADS * M_HEADDIM
M_GROUPS = 2
M_STATE = 128
M_XBC = M_INNER + 2 * M_GROUPS * M_STATE
P_WINDOWS = (2, 4, 8, 16)
P_GROUP = 128
P_WIDTH = 512
F_WIDTH = 512
N_BRANCH = 4
D_FF = 5632
FFN_CONV = 3

W_A, W_Z, W_XBC, W_DT, W_P, W_F, W_GATE = 0, 2560, 3584, 5120, 5152, 5664, 6176
W_END = W_GATE + N_BRANCH * D_MODEL

U_GATE = 0
U_Q, U_I, U_FF, U_FB, U_G = 8192, 8704, 9216, 9728, 10240
U_Z = 10752
U_XBC = 11776
U_P = 13312
U_F = 13824
U_DT = 14336
NU = 14592
ACT_W = A_WIDTH + M_INNER + P_WIDTH + F_WIDTH

LANE = 128
SUBLANE = 8
PACK = 16
CHUNK = 128
VMEM_LIMIT = 56 * 1024 * 1024


def _cp(sem):
    return pltpu.CompilerParams(dimension_semantics=sem, vmem_limit_bytes=VMEM_LIMIT)


def _sigmoid(x):
    return 0.5 * jnp.tanh(0.5 * x) + 0.5


def _silu(x):
    return x * _sigmoid(x)


def _softplus(x):
    return jnp.maximum(x, 0.0) + jnp.log(1.0 + jnp.exp(-jnp.abs(x)))


def _dot(a, b):
    return jnp.dot(a, b, preferred_element_type=F32)


def _dot_nt(a, b):
    return lax.dot_general(a, b, (((1,), (1,)), ((), ())), preferred_element_type=F32)


def _dot_exact(a, b):
    return jnp.dot(a, b, preferred_element_type=F32, precision=lax.Precision.HIGHEST)


def _row_is_ctx(row0, tm, t_lat):
    row = row0 + lax.broadcasted_iota(jnp.int32, (tm, 1), 0)
    return row >= t_lat


def _ada_kernel(c_ref, w_ref, b_ref, o_ref):
    s = _silu(c_ref[...])
    o_ref[0] = _dot(s.astype(BF16), w_ref[0].astype(BF16)) + b_ref[0]


def _ada(c_all, w_ada, b_ada, tn=1024):
    nl, d, n6 = w_ada.shape
    return pl.pallas_call(
        _ada_kernel,
        grid=(nl, n6 // tn),
        in_specs=[
            pl.BlockSpec((SUBLANE, d), lambda l, j: (0, 0)),
            pl.BlockSpec((1, d, tn), lambda l, j: (l, 0, j)),
            pl.BlockSpec((1, 1, tn), lambda l, j: (l, 0, j)),
        ],
        out_specs=pl.BlockSpec((1, SUBLANE, tn), lambda l, j: (l, 0, j)),
        out_shape=jax.ShapeDtypeStruct((nl, SUBLANE, n6), F32),
        compiler_params=_cp(("arbitrary", "arbitrary")),
        name="ada",
    )(c_all, w_ada, b_ada.reshape(nl, 1, n6))


def _norm_mod(x, is_ctx, nw, shl, scl, shc, scc):
    ms = jnp.mean(x * x, axis=-1, keepdims=True)
    y = x * lax.rsqrt(ms + EPS) * nw
    return y * (1.0 + jnp.where(is_ctx, scc, scl)) + jnp.where(is_ctx, shc, shl)


def _norm_mod_tile(x_ref, h_ref, tm, row0, t_lat, nw, shl, scl, shc, scc):
    gain_l, gain_c = nw * (1.0 + scl), nw * (1.0 + scc)

    def body(r, carry):
        rows = pl.ds(pl.multiple_of(r * PACK, PACK), PACK)
        is_ctx = row0 + r * PACK >= t_lat
        x = x_ref[0, rows, :]
        ms = jnp.mean(x * x, axis=-1, keepdims=True)
        y = x * lax.rsqrt(ms + EPS) * jnp.where(is_ctx, gain_c, gain_l) + jnp.where(is_ctx, shc, shl)
        h_ref[rows, :] = y.astype(BF16)
        return carry

    lax.fori_loop(0, tm // PACK, body, 0, unroll=3)


def _nmm_kernel(x_ref, nw_ref, shl_ref, scl_ref, shc_ref, scc_ref, w_ref, o_ref, h_ref, *, tm, t_lat):
    @pl.when(pl.program_id(2) == 0)
    def _():
        _norm_mod_tile(x_ref, h_ref, tm, pl.program_id(1) * tm, t_lat,
                       nw_ref[...], shl_ref[0], scl_ref[0], shc_ref[...], scc_ref[...])

    o_ref[0] = _dot(h_ref[...], w_ref[...]).astype(o_ref.dtype)


def _norm_mod_matmul(x, nw, shl, scl, shc, scc, w, *, t_lat, tm, tn):
    b, r, d = x.shape
    n = w.shape[1]
    vec_l = pl.BlockSpec((1, 1, d), lambda bb, i, j: (bb, 0, 0))
    vec_c = pl.BlockSpec((1, d), lambda bb, i, j: (0, 0))
    return pl.pallas_call(
        functools.partial(_nmm_kernel, tm=tm, t_lat=t_lat),
        grid=(b, r // tm, n // tn),
        in_specs=[
            pl.BlockSpec((1, tm, d), lambda bb, i, j: (bb, i, 0)),
            vec_c, vec_l, vec_l, vec_c, vec_c,
            pl.BlockSpec((d, tn), lambda bb, i, j: (0, j)),
        ],
        out_specs=pl.BlockSpec((1, tm, tn), lambda bb, i, j: (bb, i, j)),
        out_shape=jax.ShapeDtypeStruct((b, r, n), BF16),
        scratch_shapes=[pltpu.VMEM((tm, d), BF16)],
        compiler_params=_cp(("arbitrary", "arbitrary", "arbitrary")),
        name="norm_mod_matmul",
    )(x, nw, shl, scl, shc, scc, w)


def _halo_specs(tm, tc, col_of, r_tot, halo):
    nb = tm // halo
    last = r_tot // halo - 1
    prev = pl.BlockSpec((1, halo, tc), lambda b, i, j: (b, jnp.maximum(i * nb - 1, 0), col_of(j)))
    nxt = pl.BlockSpec((1, halo, tc), lambda b, i, j: (b, jnp.minimum((i + 1) * nb, last), col_of(j)))
    return prev, nxt


def _conv_silu_kernel(p_ref, m_ref, n_ref, w_ref, b_ref, o_ref, ext_ref, *, tm, ksize, t_lat, r_tot):
    main = m_ref[0].astype(F32)
    ext_ref[0:PACK, :] = p_ref[0].astype(F32)
    ext_ref[PACK:PACK + tm, :] = main
    ext_ref[PACK + tm:2 * PACK + tm, :] = n_ref[0].astype(F32)
    i = pl.program_id(1)
    w = w_ref[...]
    centre = main * w[ksize // 2:ksize // 2 + 1, :] + b_ref[...]
    taps = [(k, k - ksize // 2) for k in range(ksize) if k != ksize // 2]
    has_edge = (i == 0) | ((i + 1) * tm >= t_lat)

    @pl.when(has_edge)
    def _():
        t = i * tm + lax.broadcasted_iota(jnp.int32, (tm, 1), 0)
        is_ctx = t >= t_lat
        lo = jnp.where(is_ctx, t_lat, 0)
        hi = jnp.where(is_ctx, r_tot, t_lat)
        acc = centre
        for k, dk in taps:
            n = t + dk
            acc = acc + jnp.where((n >= lo) & (n < hi), ext_ref[pl.ds(PACK + dk, tm), :], 0.0) * w[k:k + 1, :]
        o_ref[0] = _silu(acc).astype(o_ref.dtype)

    @pl.when(jnp.logical_not(has_edge))
    def _():
        acc = centre
        for k, dk in taps:
            acc = acc + ext_ref[pl.ds(PACK + dk, tm), :] * w[k:k + 1, :]
        o_ref[0] = _silu(acc).astype(o_ref.dtype)


def _conv_silu(u, col0, width, w, bias, *, t_lat, tm, tc=512):
    b, r, _ = u.shape
    ksize = w.shape[0]
    cb = col0 // tc
    prev, nxt = _halo_specs(tm, tc, lambda j: cb + j, r, PACK)
    return pl.pallas_call(
        functools.partial(_conv_silu_kernel, tm=tm, ksize=ksize, t_lat=t_lat, r_tot=r),
        grid=(b, r // tm, width // tc),
        in_specs=[
            prev,
            pl.BlockSpec((1, tm, tc), lambda bb, i, j: (bb, i, cb + j)),
            nxt,
            pl.BlockSpec((ksize, tc), lambda bb, i, j: (0, j)),
            pl.BlockSpec((1, tc), lambda bb, i, j: (0, j)),
        ],
        out_specs=pl.BlockSpec((1, tm, tc), lambda bb, i, j: (bb, i, j)),
        out_shape=jax.ShapeDtypeStruct((b, r, width), BF16),
        scratch_shapes=[pltpu.VMEM((tm + 2 * PACK, tc), F32)],
        compiler_params=_cp(("arbitrary", "arbitrary", "arbitrary")),
        name="conv_silu",
    )(u, u, u, w, bias.reshape(1, width))


def _level_masks(c):
    t = np.arange(c)[:, None]
    s = np.arange(c)[None, :]
    ms = [(t == s)]
    h = 1
    while h < c:
        g = 2 * h
        ms.append((t // g == s // g) & ((t % g) >= h) & ((s % g) < h))
        h *= 2
    fwd = np.stack(ms).astype(np.float32)
    return np.stack([fwd, np.transpose(fwd, (0, 2, 1))])


def _hgrn_dir(q_raw, v, f_raw, lbp, masks_ref, d, st_ref, o_ref, rev, c, nheads, hd):
    width = nheads * hd
    llb, l1m, oml = lbp[0:1, :], lbp[1:2, :], lbp[2:3, :]
    q = _silu(q_raw)
    e = jnp.exp(-jnp.abs(f_raw))
    inv = 1.0 / (1.0 + e)
    k = oml * jnp.where(f_raw >= 0, e * inv, inv)
    a = l1m + jnp.minimum(f_raw, 0.0) - jnp.log(1.0 + e)
    logf = jnp.maximum(llb, a) + jnp.log(1.0 + jnp.exp(-jnp.abs(llb - a)))

    t_idx = lax.broadcasted_iota(jnp.int32, (c, 1), 0)
    p = logf
    tot = logf
    levels = []
    h = 1
    while h < SUBLANE:
        up = (t_idx & h) != 0
        tgt = jnp.logical_not(up) if rev else up
        levels.append(jnp.exp(jnp.where(tgt, p, tot - p)))
        sib = jnp.where(up, pltpu.roll(tot, h, 0), pltpu.roll(tot, c - h, 0))
        p = p + jnp.where(tgt, sib, 0.0)
        tot = tot + sib
        h *= 2
    nblk = c // SUBLANE
    pb = [p[b * SUBLANE:(b + 1) * SUBLANE] for b in range(nblk)]
    tb = [tot[b * SUBLANE:(b + 1) * SUBLANE] for b in range(nblk)]
    m = 1
    while m < nblk:
        tgt = [((b // m) % 2 == 1) != rev for b in range(nblk)]
        levels.append(jnp.exp(jnp.concatenate([pb[b] if tgt[b] else tb[b] - pb[b] for b in range(nblk)], axis=0)))
        pb = [pb[b] + tb[b ^ m] if tgt[b] else pb[b] for b in range(nblk)]
        pair = {}
        for b in range(nblk):
            if min(b, b ^ m) not in pair:
                pair[min(b, b ^ m)] = tb[b] + tb[b ^ m]
        tb = [pair[min(b, b ^ m)] for b in range(nblk)]
        m *= 2
    p = jnp.concatenate(pb, axis=0)
    eb = jnp.exp(p)
    ek = jnp.exp(jnp.concatenate([tb[b] - pb[b] for b in range(nblk)], axis=0))
    etot = jnp.exp(tb[0][0:1, :])

    for hh in range(nheads):
        sl = slice(hh * hd, (hh + 1) * hd)
        qh, kh, vh = q[:, sl], k[:, sl], v[:, sl]
        a_mat = _dot_nt(qh.astype(BF16), kh.astype(BF16)) * masks_ref[d, 0]
        for li, lev in enumerate(levels):
            eh = lev[:, sl]
            a_mat = a_mat + _dot_nt((qh * eh).astype(BF16), (kh * eh).astype(BF16)) * masks_ref[d, li + 1]
        st = st_ref[d, hh]
        o = _dot(a_mat.astype(BF16), vh.astype(BF16))
        o = o + _dot_nt((qh * eb[:, sl]).astype(BF16), st.astype(BF16))
        o_ref[0, :, sl] = o.astype(o_ref.dtype)
        khat = kh * ek[:, sl]
        st_ref[d, hh] = st * etot[:, sl] + _dot(vh.T.astype(BF16), khat.astype(BF16))


def _hgrn_kernel(qf_ref, if_ref, ff_ref, qb_ref, ib_ref, fb_ref, lbf_ref, lbb_ref, masks_ref,
                 of_ref, ob_ref, st_ref, *, c, nheads, hd):
    @pl.when(pl.program_id(1) == 0)
    def _():
        st_ref[...] = jnp.zeros_like(st_ref)

    ld = lambda ref: ref[0].astype(F32)
    _hgrn_dir(ld(qf_ref), ld(if_ref), ld(ff_ref), lbf_ref[...], masks_ref, 0, st_ref, of_ref, False, c, nheads, hd)
    _hgrn_dir(ld(qb_ref), ld(ib_ref), ld(fb_ref), lbb_ref[...], masks_ref, 1, st_ref, ob_ref, True, c, nheads, hd)


def _scan_chunk_maps(n_lat, n_ctx):
    n = n_lat + n_ctx
    fwd = lambda j: (j + n_lat) % n
    bwd = lambda j: n - 1 - j
    return n, fwd, bwd


def _hgrn_scan(u, cq, ci, cff, cfb, lbp_f, lbp_b, *, t_lat, c=CHUNK, nheads=A_HEADS, hd=A_DIM):
    b, r, _ = u.shape
    width = nheads * hd
    n, fwd, bwd = _scan_chunk_maps(t_lat // c, (r - t_lat) // c)
    masks = jnp.asarray(_level_masks(c))

    def spec(col, cm):
        return pl.BlockSpec((1, c, width), lambda bb, j: (bb, cm(j), col // width))

    const2 = pl.BlockSpec((3, width), lambda bb, j: (0, 0))
    out_shape = jax.ShapeDtypeStruct((b, r, width), BF16)
    return pl.pallas_call(
        functools.partial(_hgrn_kernel, c=c, nheads=nheads, hd=hd),
        grid=(b, n),
        in_specs=[spec(cq, fwd), spec(ci, fwd), spec(cff, fwd), spec(cq, bwd), spec(ci, bwd), spec(cfb, bwd),
                  const2, const2,
                  pl.BlockSpec(masks.shape, lambda bb, j: (0, 0, 0, 0))],
        out_specs=[pl.BlockSpec((1, c, width), lambda bb, j: (bb, fwd(j), 0)),
                   pl.BlockSpec((1, c, width), lambda bb, j: (bb, bwd(j), 0))],
        out_shape=[out_shape, out_shape],
        scratch_shapes=[pltpu.VMEM((2, nheads, hd, hd), F32)],
        compiler_params=_cp(("arbitrary", "arbitrary")),
        name="hgrn2_scan",
    )(u, u, u, u, u, u, lbp_f, lbp_b, masks)


def _ssd_dir(xm, bm, cm, dtc_raw, dtr_raw, prow, pcol, tri_ref, d, st_ref, o_ref, rev, c):
    nh = M_HEADS
    dt_c = _softplus(dtc_raw + prow[0:1, :])
    lf_c = dt_c * prow[1:2, :]
    dt_r = _softplus(dtr_raw + pcol[:, 0:1])[d * nh:(d + 1) * nh]
    lf_r = dt_r * pcol[d * nh:(d + 1) * nh, 1:2]
    lo, up = tri_ref[0], tri_ref[1]
    if rev:
        b_c = _dot_exact(up, lf_c)
        b_r = _dot_exact(lf_r, lo)
        causal = up > 0.5
        btot = b_c[0:1, :]
    else:
        b_c = _dot_exact(lo, lf_c)
        b_r = _dot_exact(lf_r, up)
        causal = lo > 0.5
        btot = b_c[c - 1:c, :]
    lane = lax.broadcasted_iota(jnp.int32, (1, LANE), 1)
    left = lane < M_HEADDIM
    bd_mask = jnp.concatenate([jnp.broadcast_to(left, (M_STATE, LANE)),
                               jnp.broadcast_to(jnp.logical_not(left), (M_STATE, LANE))], axis=0)
    for g in range(M_GROUPS):
        cg = cm[:, g * M_STATE:(g + 1) * M_STATE]
        bg_t = bm[:, g * M_STATE:(g + 1) * M_STATE].T
        gmat = _dot(cg.astype(BF16), bg_t.astype(BF16))
        for pp in range(nh // (2 * M_GROUPS)):
            pair = g * (nh // (2 * M_GROUPS)) + pp
            lhs, qs, kts, decs = [], [], [], []
            for h in (2 * pair, 2 * pair + 1):
                hc = d * nh + h
                bcol = b_c[:, hc:hc + 1]
                brow = b_r[h:h + 1, :]
                dtrow = dt_r[h:h + 1, :]
                dec = jnp.where(causal, jnp.exp(bcol - brow), 0.0) * dtrow
                lhs.append(gmat * dec)
                qs.append(cg * jnp.exp(bcol))
                bt = btot[:, hc:hc + 1]
                kts.append(bg_t * (jnp.exp(bt - brow) * dtrow))
                decs.append(jnp.broadcast_to(jnp.exp(bt), (M_STATE, LANE)))
            xp = xm[:, pair * LANE:(pair + 1) * LANE]
            rhs = jnp.concatenate([jnp.where(left, xp, 0.0), jnp.where(left, 0.0, xp)], axis=0)
            st = st_ref[d, pair]
            o = _dot(jnp.concatenate(lhs, axis=1).astype(BF16), rhs.astype(BF16))
            o = o + _dot(jnp.concatenate(qs, axis=1).astype(BF16), st.astype(BF16))
            o_ref[0, :, pair * LANE:(pair + 1) * LANE] = o.astype(o_ref.dtype)
            upd = _dot(jnp.concatenate(kts, axis=0).astype(BF16), xp.astype(BF16))
            st_ref[d, pair] = st * jnp.concatenate(decs, axis=0) + jnp.where(bd_mask, upd, 0.0)


def _ssd_kernel(xf_ref, bf_ref, cf_ref, dcf_ref, drf_ref, xb_ref, bb_ref, cb_ref, dcb_ref, drb_ref,
                prow_ref, pcol_ref, tri_ref, of_ref, ob_ref, st_ref, *, c):
    @pl.when(pl.program_id(1) == 0)
    def _():
        st_ref[...] = jnp.zeros_like(st_ref)

    ld = lambda ref: ref[0].astype(F32)
    _ssd_dir(ld(xf_ref), ld(bf_ref), ld(cf_ref), ld(dcf_ref), drf_ref[0], prow_ref[...], pcol_ref[...],
             tri_ref, 0, st_ref, of_ref, False, c)
    _ssd_dir(ld(xb_ref), ld(bb_ref), ld(cb_ref), ld(dcb_ref), drb_ref[0], prow_ref[...], pcol_ref[...],
             tri_ref, 1, st_ref, ob_ref, True, c)


def _tri_mats(c):
    t = np.arange(c)[:, None]
    s = np.arange(c)[None, :]
    return np.stack([(s <= t), (s >= t)]).astype(np.float32)


def _ssd_scan(xbc, u, dt_t, prow, pcol, *, t_lat, c=CHUNK):
    b, r, _ = xbc.shape
    n, fwd, bwd = _scan_chunk_maps(t_lat // c, (r - t_lat) // c)
    gs = M_GROUPS * M_STATE
    tri = jnp.asarray(_tri_mats(c))

    def specs(cm):
        return [pl.BlockSpec((1, c, M_INNER), lambda bb, j: (bb, cm(j), 0)),
                pl.BlockSpec((1, c, gs), lambda bb, j: (bb, cm(j), M_INNER // gs)),
                pl.BlockSpec((1, c, gs), lambda bb, j: (bb, cm(j), M_INNER // gs + 1)),
                pl.BlockSpec((1, c, LANE), lambda bb, j: (bb, cm(j), U_DT // LANE)),
                pl.BlockSpec((1, 2 * M_HEADS, c), lambda bb, j: (bb, 0, cm(j)))]

    out_shape = jax.ShapeDtypeStruct((b, r, M_INNER), BF16)
    return pl.pallas_call(
        functools.partial(_ssd_kernel, c=c),
        grid=(b, n),
        in_specs=specs(fwd) + specs(bwd) + [
            pl.BlockSpec(prow.shape, lambda bb, j: (0, 0)),
            pl.BlockSpec(pcol.shape, lambda bb, j: (0, 0)),
            pl.BlockSpec(tri.shape, lambda bb, j: (0, 0, 0))],
        out_specs=[pl.BlockSpec((1, c, M_INNER), lambda bb, j: (bb, fwd(j), 0)),
                   pl.BlockSpec((1, c, M_INNER), lambda bb, j: (bb, bwd(j), 0))],
        out_shape=[out_shape, out_shape],
        scratch_shapes=[pltpu.VMEM((2, M_HEADS // 2, 2 * M_STATE, LANE), F32)],
        compiler_params=_cp(("arbitrary", "arbitrary")),
        name="ssd_scan",
    )(xbc, xbc, xbc, u, dt_t, xbc, xbc, xbc, u, dt_t, prow, pcol, tri)


def _band_mats(windows, tb, period):
    t = np.arange(tb)[:, None]
    s = np.arange(tb)[None, :]
    out = []
    for w in windows:
        left = w // 2
        right = w - 1 - left
        out.append((t // period == s // period) & (s >= t - left) & (s <= t + right))
    return np.stack(out).astype(np.float32)


def _clip_count(idx, n, left, right):
    return jnp.minimum(idx + right + 1, n) - jnp.maximum(idx - left, 0)


def _pool_kernel(z_ref, band_ref, gw_ref, sc_ref, *rest, n, tb, grid_w, windows):
    o_ref, pad_ref = rest[-2:]
    g = pl.program_id(1)
    band = band_ref[0].astype(BF16)
    shift = int(math.log2(grid_w)) if grid_w else 0
    for gi, w in enumerate(windows):
        @pl.when(g == gi)
        def _(w=w):
            left = w // 2
            right = w - 1 - left
            wr = w if grid_w else 1
            top = left * grid_w
            if grid_w:
                pad_ref[0:top, :] = jnp.zeros((top, LANE), F32)
                if right:
                    pad_ref[top + n:top + n + right * grid_w, :] = jnp.zeros((right * grid_w, LANE), F32)

            def col_sum(i, carry):
                pad_ref[pl.ds(top + i * tb, tb), :] = _dot(band, z_ref[0, pl.ds(i * tb, tb), :].astype(BF16))
                return carry

            unroll = min(4, n // tb)
            lax.fori_loop(0, n // tb, col_sum, 0, unroll=unroll)

            def finish(i, carry):
                acc = pad_ref[pl.ds(i * tb, tb), :]
                for kk in range(1, wr):
                    acc = acc + pad_ref[pl.ds(i * tb + kk * grid_w, tb), :]
                t = i * tb + lax.broadcasted_iota(jnp.int32, (tb, 1), 0)
                if grid_w:
                    cnt = (_clip_count(lax.shift_right_logical(t, shift), n // grid_w, left, right)
                           * _clip_count(t & (grid_w - 1), grid_w, left, right))
                else:
                    cnt = _clip_count(t, n, left, right)
                y = acc / cnt.astype(F32) - z_ref[0, pl.ds(i * tb, tb), :].astype(F32)
                out = _dot(y.astype(BF16), gw_ref[0].astype(BF16)) * sc_ref[0]
                o_ref[0, pl.ds(i * tb, tb), :] = out.astype(o_ref.dtype)
                return carry

            lax.fori_loop(0, n // tb, finish, 0, unroll=unroll)


def _pool(u, dst, group_w, scale, *, row0, n, grid_w):
    b, r, _ = u.shape
    tb = 2 * grid_w if grid_w else n
    bands = jnp.asarray(_band_mats(P_WINDOWS, tb, grid_w if grid_w else n))
    rb = row0 // n
    pad_rows = n + (max(P_WINDOWS) - 1) * grid_w
    return pl.pallas_call(
        functools.partial(_pool_kernel, n=n, tb=tb, grid_w=grid_w, windows=P_WINDOWS),
        grid=(b, len(P_WINDOWS)),
        in_specs=[
            pl.BlockSpec((1, n, P_GROUP), lambda bb, g: (bb, rb, U_P // P_GROUP + g)),
            pl.BlockSpec((1, tb, tb), lambda bb, g: (g, 0, 0)),
            pl.BlockSpec((1, P_GROUP, P_GROUP), lambda bb, g: (g, 0, 0)),
            pl.BlockSpec((1, 1, P_GROUP), lambda bb, g: (g, 0, 0)),
        ] + ([] if dst is None else [pl.BlockSpec(memory_space=pl.ANY)]),
        out_specs=pl.BlockSpec((1, n, P_GROUP), lambda bb, g: (bb, rb, g)),
        out_shape=jax.ShapeDtypeStruct((b, r, P_WIDTH), BF16),
        scratch_shapes=[pltpu.VMEM((pad_rows, LANE), F32)],
        input_output_aliases={} if dst is None else {4: 0},
        compiler_params=_cp(("arbitrary", "arbitrary")),
        name="pool_grid" if grid_w else "pool_seq",
    )(u, bands, group_w, scale.reshape(len(P_WINDOWS), 1, P_GROUP), *(() if dst is None else (dst,)))


def _chan_dft_mats():
    k = np.arange(F_WIDTH // 4)
    ang = 2.0 * np.pi * ((k[:, None] * k[None, :]) % len(k)) / len(k)
    eye = np.eye(4)
    w = np.concatenate([np.kron(eye, np.cos(ang)), np.kron(eye, np.sin(ang))], axis=1)
    hi = w.astype(np.float32).astype(BF16)
    lo = (w - hi.astype(np.float64)).astype(np.float32).astype(BF16)
    return jnp.asarray(hi), jnp.asarray(lo)


def _chan_dft_kernel(z_ref, wh_ref, wl_ref, o_ref):
    z = z_ref[0]
    o_ref[0] = (_dot(z, wh_ref[...]) + _dot(z, wl_ref[...])).astype(o_ref.dtype)


def _chan_dft(u, *, tm):
    b, r, _ = u.shape
    wh, wl = _chan_dft_mats()
    wspec = pl.BlockSpec(wh.shape, lambda bb, i: (0, 0))
    return pl.pallas_call(
        _chan_dft_kernel,
        grid=(b, r // tm),
        in_specs=[pl.BlockSpec((1, tm, F_WIDTH), lambda bb, i: (bb, i, U_F // F_WIDTH)), wspec, wspec],
        out_specs=pl.BlockSpec((1, tm, 2 * F_WIDTH), lambda bb, i: (bb, i, 0)),
        out_shape=jax.ShapeDtypeStruct((b, r, 2 * F_WIDTH), BF16),
        compiler_params=_cp(("arbitrary", "arbitrary")),
        name="chan_dft",
    )(u, wh, wl)


def _dft_cols(n, ncols):
    k = jnp.arange(n, dtype=jnp.int32)[:, None]
    j = jnp.arange(ncols, dtype=jnp.int32)[None, :]
    ang = ((k * j) & (n - 1)).astype(F32) * (2.0 * math.pi / n)
    return jnp.cos(ang), jnp.sin(ang)


ROT_PERIOD = 16


def _rot_patterns(n, tk):
    assert ROT_PERIOD % (n // tk) == 0
    k = np.arange(ROT_PERIOD)[None, :, None]
    t0 = (np.arange(n // tk) * tk)[:, None, None]
    ang = 2.0 * np.pi * ((k * t0) % n) / n
    shape = (n // tk, ROT_PERIOD, tk)
    return (jnp.asarray(np.broadcast_to(np.cos(ang), shape), F32), jnp.asarray(np.broadcast_to(np.sin(ang), shape), F32))


def _time_dft_kernel(cb_ref, sb_ref, pc_ref, ps_ref, zc_ref, zs_ref, *rest, nb, tm, tk, scale):
    o_ref, acc_ref = rest[-2:]
    kk = pl.program_id(1)

    @pl.when(kk == 0)
    def _():
        acc_ref[...] = jnp.zeros_like(acc_ref)

    cb = cb_ref[...].reshape(tm // ROT_PERIOD, ROT_PERIOD, tk)
    sb = sb_ref[...].reshape(tm // ROT_PERIOD, ROT_PERIOD, tk)
    ca, sa = pc_ref[...], ps_ref[...]
    ct = (ca * cb - sa * sb).reshape(tm, tk).astype(BF16)
    st = (sa * cb + ca * sb).reshape(tm, tk).astype(BF16)
    for bb in range(nb):
        acc_ref[bb] += _dot(ct, zc_ref[bb]) - _dot(st, zs_ref[bb])

    @pl.when(kk == pl.num_programs(1) - 1)
    def _():
        o_ref[...] = (acc_ref[...] * scale).astype(o_ref.dtype)


def _time_dft(zcs, dst, *, row0, n, tm, tk):
    b, r, _ = zcs.shape
    scale = 1.0 / math.sqrt(n * (F_WIDTH // 4))
    cb, sb = _dft_cols(n, tk)
    pc, ps = _rot_patterns(n, tk)
    pat = pl.BlockSpec((None, ROT_PERIOD, tk), lambda i, k: (k, 0, 0))
    return pl.pallas_call(
        functools.partial(_time_dft_kernel, nb=b, tm=tm, tk=tk, scale=scale),
        grid=(n // tm, n // tk),
        in_specs=[
            pl.BlockSpec((tm, tk), lambda i, k: (i, 0)),
            pl.BlockSpec((tm, tk), lambda i, k: (i, 0)),
            pat, pat,
            pl.BlockSpec((b, tk, F_WIDTH), lambda i, k: (0, row0 // tk + k, 0)),
            pl.BlockSpec((b, tk, F_WIDTH), lambda i, k: (0, row0 // tk + k, 1)),
        ] + ([] if dst is None else [pl.BlockSpec(memory_space=pl.ANY)]),
        out_specs=pl.BlockSpec((b, tm, F_WIDTH), lambda i, k: (0, row0 // tm + i, 0)),
        out_shape=jax.ShapeDtypeStruct((b, r, F_WIDTH), BF16),
        scratch_shapes=[pltpu.VMEM((b, tm, F_WIDTH), F32)],
        input_output_aliases={} if dst is None else {6: 0},
        compiler_params=_cp(("arbitrary", "arbitrary")),
        name="time_dft_%d" % n,
    )(cb, sb, pc, ps, zcs, zcs, *(() if dst is None else (dst,)))


def _merge_kernel(oaf_ref, oab_ref, g_ref, omf_ref, omb_ref, xm_ref, z0_ref, z1_ref, pm_ref, fm_ref,
                  anw_ref, dsk_ref, mnw_ref, wbr_ref, g0_ref, g1_ref, g2_ref, g3_ref, o_ref, act_ref):
    @pl.when(pl.program_id(2) == 0)
    def _():
        ld = lambda ref: ref[0].astype(F32)
        oa = ld(oaf_ref) + ld(oab_ref)
        gate = _silu(ld(g_ref))
        for hh in range(A_HEADS):
            sl = slice(hh * A_DIM, (hh + 1) * A_DIM)
            oh = oa[:, sl]
            ms = jnp.mean(oh * oh, axis=-1, keepdims=True)
            act_ref[:, sl] = (oh * lax.rsqrt(ms + EPS) * anw_ref[...] * gate[:, sl]).astype(BF16)
        y = ld(omf_ref) + ld(omb_ref) + dsk_ref[...] * ld(xm_ref)
        z = jnp.concatenate([ld(z0_ref), ld(z1_ref)], axis=1)
        y = y * _silu(z)
        ms = jnp.mean(y * y, axis=-1, keepdims=True)
        act_ref[:, A_WIDTH:A_WIDTH + M_INNER] = (y * lax.rsqrt(ms + EPS) * mnw_ref[...]).astype(BF16)
        act_ref[:, A_WIDTH + M_INNER:A_WIDTH + M_INNER + P_WIDTH] = pm_ref[0].astype(BF16)
        act_ref[:, A_WIDTH + M_INNER + P_WIDTH:] = fm_ref[0].astype(BF16)

    offs = (0, A_WIDTH, A_WIDTH + M_INNER, A_WIDTH + M_INNER + P_WIDTH, ACT_W)
    acc = None
    for br, gr in enumerate((g0_ref, g1_ref, g2_ref, g3_ref)):
        yb = _dot(act_ref[:, offs[br]:offs[br + 1]], wbr_ref[offs[br]:offs[br + 1], :])
        term = _sigmoid(gr[0].astype(F32)) * yb
        acc = term if acc is None else acc + term
    o_ref[0] = acc.astype(o_ref.dtype)


def _merge(u, oaf, oab, omf, omb, xbc, pm, fm, anw, dsk, mnw, wbr, *, tm, tn=512):
    b, r, _ = u.shape

    def rows(width, cb):
        return pl.BlockSpec((1, tm, width), lambda bb, i, j: (bb, i, cb))

    def vec(width):
        return pl.BlockSpec((1, width), lambda bb, i, j: (0, 0))

    def gate(br):
        return pl.BlockSpec((1, tm, tn), lambda bb, i, j: (bb, i, (U_GATE + br * D_MODEL) // tn + j))

    return pl.pallas_call(
        _merge_kernel,
        grid=(b, r // tm, D_MODEL // tn),
        in_specs=[rows(A_WIDTH, 0), rows(A_WIDTH, 0), rows(A_WIDTH, U_G // A_WIDTH),
                  rows(M_INNER, 0), rows(M_INNER, 0), rows(M_INNER, 0),
                  rows(512, U_Z // 512), rows(512, U_Z // 512 + 1),
                  rows(P_WIDTH, 0), rows(F_WIDTH, 0),
                  vec(A_DIM), vec(M_INNER), vec(M_INNER),
                  pl.BlockSpec((ACT_W, tn), lambda bb, i, j: (0, j)),
                  gate(0), gate(1), gate(2), gate(3)],
        out_specs=pl.BlockSpec((1, tm, tn), lambda bb, i, j: (bb, i, j)),
        out_shape=jax.ShapeDtypeStruct((b, r, D_MODEL), BF16),
        scratch_shapes=[pltpu.VMEM((tm, ACT_W), BF16)],
        compiler_params=_cp(("arbitrary", "arbitrary", "arbitrary")),
        name="merge",
    )(oaf, oab, u, omf, omb, xbc, u, u, pm, fm, anw, dsk, mnw, wbr, u, u, u, u)


def _proj_resid_kernel(y_ref, w_ref, x_ref, gl_ref, gc_ref, o_ref, *, tm, t_lat):
    is_ctx = _row_is_ctx(pl.program_id(1) * tm, tm, t_lat)
    gate = jnp.where(is_ctx, gc_ref[...], gl_ref[0])
    o_ref[0] = x_ref[0] + gate * _dot(y_ref[0], w_ref[...])


def _proj_resid(y, w, x, gl, gc, *, t_lat, tm):
    b, r, d = x.shape
    k = y.shape[2]
    return pl.pallas_call(
        functools.partial(_proj_resid_kernel, tm=tm, t_lat=t_lat),
        grid=(b, r // tm),
        in_specs=[pl.BlockSpec((1, tm, k), lambda bb, i: (bb, i, 0)),
                  pl.BlockSpec((k, d), lambda bb, i: (0, 0)),
                  pl.BlockSpec((1, tm, d), lambda bb, i: (bb, i, 0)),
                  pl.BlockSpec((1, 1, d), lambda bb, i: (bb, 0, 0)),
                  pl.BlockSpec((1, d), lambda bb, i: (0, 0))],
        out_specs=pl.BlockSpec((1, tm, d), lambda bb, i: (bb, i, 0)),
        out_shape=jax.ShapeDtypeStruct((b, r, d), F32),
        compiler_params=_cp(("arbitrary", "arbitrary")),
        name="proj_resid",
    )(y, w, x, gl, gc)


def _conv3(ext, s_ref, w, bias, tm, m_prev, m_next):
    main = ext[0:tm]
    s_ref[SUBLANE:SUBLANE + tm, :] = main
    s_ref[SUBLANE - 1:SUBLANE, :] = ext[tm + SUBLANE - 1:tm + SUBLANE]
    s_ref[SUBLANE + tm:SUBLANE + tm + 1, :] = ext[tm + SUBLANE:tm + SUBLANE + 1]
    prev = s_ref[pl.ds(SUBLANE - 1, tm), :]
    nxt = s_ref[pl.ds(SUBLANE + 1, tm), :]
    return main * w[1:2, :] + (prev * m_prev) * w[0:1, :] + (nxt * m_next) * w[2:3, :] + bias


def _ffn_kernel(xp_ref, x_ref, xn_ref, nw_ref, shl_ref, scl_ref, shc_ref, scc_ref, gl_ref, gc_ref, fnw_ref,
                wa_ref, wb_ref, cwa_ref, cwb_ref, ba_ref, bb_ref, wd_ref, o_ref, h_ref, sa_ref, sb_ref,
                *, tm, t_lat, r_tot, final_norm):
    i = pl.program_id(1)
    kk = pl.program_id(2)
    row0 = i * tm

    @pl.when(kk == 0)
    def _():
        vecs = (nw_ref[...], shl_ref[0], scl_ref[0], shc_ref[...], scc_ref[...])
        _norm_mod_tile(x_ref, h_ref, tm, row0, t_lat, *vecs)
        halo = jnp.concatenate([xp_ref[0], xn_ref[0]], axis=0)
        hrow = lax.broadcasted_iota(jnp.int32, (2 * SUBLANE, 1), 0)
        grow = jnp.where(hrow < SUBLANE, row0 - SUBLANE + hrow, row0 + tm - SUBLANE + hrow)
        h_ref[tm:tm + 2 * SUBLANE, :] = _norm_mod(halo, grow >= t_lat, *vecs).astype(BF16)
        o_ref[...] = jnp.zeros_like(o_ref)

    t = row0 + lax.broadcasted_iota(jnp.int32, (tm, 1), 0)
    m_prev = jnp.where((t == 0) | (t == t_lat), 0.0, 1.0)
    m_next = jnp.where((t == t_lat - 1) | (t == r_tot - 1), 0.0, 1.0)
    h = h_ref[...]
    a = _conv3(_dot(h, wa_ref[...]), sa_ref, cwa_ref[...], ba_ref[...], tm, m_prev, m_next)
    g = _conv3(_dot(h, wb_ref[...]), sb_ref, cwb_ref[...], bb_ref[...], tm, m_prev, m_next)
    o_ref[0] += _dot((_silu(a) * g).astype(BF16), wd_ref[...])

    @pl.when(kk == pl.num_programs(2) - 1)
    def _():
        gate = jnp.where(_row_is_ctx(row0, tm, t_lat), gc_ref[...], gl_ref[0])
        y = x_ref[0] + gate * o_ref[0]
        if final_norm:
            ms = jnp.mean(y * y, axis=-1, keepdims=True)
            y = y * lax.rsqrt(ms + EPS) * fnw_ref[...]
        o_ref[0] = y


def _ffn(x, nw, shl, scl, shc, scc, gl, gc, fnw, w_up, cw, cb, wd, *, t_lat, tm, tk, final_norm, out_rows):
    b, r, d = x.shape
    nk = D_FF // tk
    x_prev, x_next = _halo_specs(tm, d, lambda k: 0, r, SUBLANE)
    vec_l = pl.BlockSpec((1, 1, d), lambda bb, i, k: (bb, 0, 0))
    vec_c = pl.BlockSpec((1, d), lambda bb, i, k: (0, 0))
    cb2 = cb.reshape(1, 2 * D_FF)
    return pl.pallas_call(
        functools.partial(_ffn_kernel, tm=tm, t_lat=t_lat, r_tot=r, final_norm=final_norm),
        grid=(b, r // tm, nk),
        in_specs=[x_prev, pl.BlockSpec((1, tm, d), lambda bb, i, k: (bb, i, 0)), x_next,
                  vec_c, vec_l, vec_l, vec_c, vec_c, vec_l, vec_c, vec_c,
                  pl.BlockSpec((d, tk), lambda bb, i, k: (0, k)),
                  pl.BlockSpec((d, tk), lambda bb, i, k: (0, nk + k)),
                  pl.BlockSpec((FFN_CONV, tk), lambda bb, i, k: (0, k)),
                  pl.BlockSpec((FFN_CONV, tk), lambda bb, i, k: (0, nk + k)),
                  pl.BlockSpec((1, tk), lambda bb, i, k: (0, k)),
                  pl.BlockSpec((1, tk), lambda bb, i, k: (0, nk + k)),
                  pl.BlockSpec((tk, d), lambda bb, i, k: (k, 0))],
        out_specs=pl.BlockSpec((1, tm, d), lambda bb, i, k: (bb, i, 0)),
        out_shape=jax.ShapeDtypeStruct((b, out_rows, d), F32),
        scratch_shapes=[pltpu.VMEM((tm + 2 * SUBLANE, d), BF16),
                        pltpu.VMEM((tm + 2 * SUBLANE, tk), F32),
                        pltpu.VMEM((tm + 2 * SUBLANE, tk), F32)],
        compiler_params=_cp(("arbitrary", "arbitrary", "arbitrary")),
        name="ffn",
    )(x, x, x, nw, shl, scl, shc, scc, gl, gc, fnw, w_up, w_up, cw, cw, cb2, cb2, wd)


def _arrange_w_in(w):
    pad = jnp.zeros((w.shape[0], NU - W_END), w.dtype)
    return jnp.concatenate([w[:, W_GATE:W_END], w[:, W_A:W_DT], w[:, W_P:W_GATE], w[:, W_DT:W_P], pad],
                           axis=1).astype(BF16)


def kernel(x, c, ctx, c_ctx, w_ada, b_ada, norm1_w, w_in, a_lb_logits, a_norm_w, w_br_a, m_conv_w, m_conv_b,
           m_dt_bias, m_a_log, m_d, m_norm_w, w_br_m, p_group_w, p_scale, w_br_p, w_br_f, w_out, norm2_w,
           w_up, ffn_conv_w, ffn_conv_b, w_down, final_norm_w):
    nb, t_lat, d = x.shape
    t_ctx = ctx.shape[1]
    depth = w_ada.shape[0]
    tm_in, tm = 1056, 768
    assert t_lat % CHUNK == 0 and t_ctx % CHUNK == 0 and (t_lat + t_ctx) % tm_in == 0 and (t_lat + t_ctx) % tm == 0

    lb_all = jnp.cumsum(jax.nn.softmax(a_lb_logits.astype(F32), axis=0), axis=0)
    lb_all = lb_all - lb_all[0]

    xs = jnp.concatenate([x, ctx], axis=1)
    c_all = jnp.concatenate([c, c_ctx[None, :], jnp.zeros((SUBLANE - nb - 1, d), F32)], axis=0)
    mod = _ada(c_all, w_ada, b_ada)

    for l in range(depth):
        last = l == depth - 1
        ml = mod[l, :nb].reshape(nb, 1, 6, d)
        mc = mod[l, nb].reshape(6, d)
        lat = lambda k: ml[:, :, k, :]
        cx = lambda k: mc[k:k + 1, :]

        u = _norm_mod_matmul(xs, norm1_w[l][None, :], lat(0), lat(1), cx(0), cx(1), _arrange_w_in(w_in[l]),
                             t_lat=t_lat, tm=tm_in, tn=2432)
        lbp = [jnp.stack([jnp.log(lb_all[l, dd]), jnp.log1p(-lb_all[l, dd]), 1.0 - lb_all[l, dd]])
               for dd in range(2)]
        oaf, oab = _hgrn_scan(u, U_Q, U_I, U_FF, U_FB, lbp[0], lbp[1], t_lat=t_lat)

        xbc = _conv_silu(u, U_XBC, M_XBC, m_conv_w[l], m_conv_b[l], t_lat=t_lat, tm=tm)
        dt_t = jnp.transpose(u[:, :, U_DT:U_DT + 2 * M_HEADS].astype(F32), (0, 2, 1))
        a_neg = -jnp.exp(m_a_log[l].astype(F32)).reshape(-1)
        dtb = m_dt_bias[l].astype(F32).reshape(-1)
        prow = jnp.zeros((2, LANE), F32).at[0, :2 * M_HEADS].set(dtb).at[1, :2 * M_HEADS].set(a_neg)
        pcol = jnp.stack([dtb, a_neg], axis=1)
        omf, omb = _ssd_scan(xbc, u, dt_t, prow, pcol, t_lat=t_lat)

        pm = _pool(u, None, p_group_w[l], p_scale[l], row0=0, n=t_lat, grid_w=GRID_W)
        pm = _pool(u, pm, p_group_w[l], p_scale[l], row0=t_lat, n=t_ctx, grid_w=0)
        zcs = _chan_dft(u, tm=tm)
        fm = _time_dft(zcs, None, row0=0, n=t_lat, tm=1024, tk=512)
        fm = _time_dft(zcs, fm, row0=t_lat, n=t_ctx, tm=t_ctx, tk=t_ctx)

        wbr = jnp.concatenate([w_br_a[l], w_br_m[l], w_br_p[l], w_br_f[l]], axis=0).astype(BF16)
        dsk = jnp.repeat(m_d[l].astype(F32), M_HEADDIM)[None, :]
        anw = a_norm_w[l].astype(F32)[None, :]
        y = _merge(u, oaf, oab, omf, omb, xbc, pm, fm, anw, dsk, m_norm_w[l][None, :], wbr, tm=tm)
        xs = _proj_resid(y, w_out[l].astype(BF16), xs, lat(2), cx(2), t_lat=t_lat, tm=tm)

        xs = _ffn(xs, norm2_w[l][None, :], lat(3), lat(4), cx(3), cx(4), lat(5), cx(5), final_norm_w[None, :],
                  w_up[l].astype(BF16), ffn_conv_w[l], ffn_conv_b[l], w_down[l].astype(BF16),
                  t_lat=t_lat, tm=tm, tk=512, final_norm=last, out_rows=t_lat if last else t_lat + t_ctx)
    return xs
```

```python
import functools
import math

import numpy as np
import jax
import jax.numpy as jnp
from jax import lax
from jax.experimental import pallas as pl
from jax.experimental.pallas import tpu as pltpu

F32 = jnp.float32
BF16 = jnp.bfloat16

D_MODEL = 2048
GRID_W = 64
EPS = 1e-6

A_HEADS = 4
A_DIM = 128
A_WIDTH = A_HEADS * A_DIM
M_HEADS = 16
M_HEADDIM = 64
M_INNER = M_HEADS * M_HEADDIM
M_GROUPS = 2
M_STATE = 128
M_XBC = M_INNER + 2 * M_GROUPS * M_STATE
P_WINDOWS = (2, 4, 8, 16)
P_GROUP = 128
P_WIDTH = 512
F_WIDTH = 512
N_BRANCH = 4
D_FF = 5632
FFN_CONV = 3

W_A, W_Z, W_XBC, W_DT, W_P, W_F, W_GATE = 0, 2560, 3584, 5120, 5152, 5664, 6176
W_END = W_GATE + N_BRANCH * D_MODEL

U_GATE = 0
U_Q, U_I, U_FF, U_FB, U_G = 8192, 8704, 9216, 9728, 10240
U_Z = 10752
U_XBC = 11776
U_P = 13312
U_F = 13824
U_DT = 14336
NU = 14592
ACT_W = A_WIDTH + M_INNER + P_WIDTH + F_WIDTH

LANE = 128
SUBLANE = 8
PACK = 16
CHUNK = 128
VMEM_LIMIT = 56 * 1024 * 1024


def _cp(sem):
    return pltpu.CompilerParams(dimension_semantics=sem, vmem_limit_bytes=VMEM_LIMIT)


def _sigmoid(x):
    return 0.5 * jnp.tanh(0.5 * x) + 0.5


def _silu(x):
    return x * _sigmoid(x)


def _softplus(x):
    return jnp.maximum(x, 0.0) + jnp.log(1.0 + jnp.exp(-jnp.abs(x)))


def _dot(a, b):
    return jnp.dot(a, b, preferred_element_type=F32)


def _dot_nt(a, b):
    return lax.dot_general(a, b, (((1,), (1,)), ((), ())), preferred_element_type=F32)


def _dot_exact(a, b):
    return jnp.dot(a, b, preferred_element_type=F32, precision=lax.Precision.HIGHEST)


def _row_is_ctx(row0, tm, t_lat):
    row = row0 + lax.broadcasted_iota(jnp.int32, (tm, 1), 0)
    return row >= t_lat


def _ada_kernel(c_ref, w_ref, b_ref, o_ref):
    s = _silu(c_ref[...])
    o_ref[0] = _dot(s.astype(BF16), w_ref[0].astype(BF16)) + b_ref[0]


def _ada(c_all, w_ada, b_ada, tn=1024):
    nl, d, n6 = w_ada.shape
    return pl.pallas_call(
        _ada_kernel,
        grid=(nl, n6 // tn),
        in_specs=[
            pl.BlockSpec((SUBLANE, d), lambda l, j: (0, 0)),
            pl.BlockSpec((1, d, tn), lambda l, j: (l, 0, j)),
            pl.BlockSpec((1, 1, tn), lambda l, j: (l, 0, j)),
        ],
        out_specs=pl.BlockSpec((1, SUBLANE, tn), lambda l, j: (l, 0, j)),
        out_shape=jax.ShapeDtypeStruct((nl, SUBLANE, n6), F32),
        compiler_params=_cp(("arbitrary", "arbitrary")),
        name="ada",
    )(c_all, w_ada, b_ada.reshape(nl, 1, n6))


def _norm_mod(x, is_ctx, nw, shl, scl, shc, scc):
    ms = jnp.mean(x * x, axis=-1, keepdims=True)
    y = x * lax.rsqrt(ms + EPS) * nw
    return y * (1.0 + jnp.where(is_ctx, scc, scl)) + jnp.where(is_ctx, shc, shl)


def _norm_mod_tile(x_ref, h_ref, tm, row0, t_lat, nw, shl, scl, shc, scc):
    gain_l, gain_c = nw * (1.0 + scl), nw * (1.0 + scc)

    def body(r, carry):
        rows = pl.ds(pl.multiple_of(r * PACK, PACK), PACK)
        is_ctx = row0 + r * PACK >= t_lat
        x = x_ref[0, rows, :]
        ms = jnp.mean(x * x, axis=-1, keepdims=True)
        y = x * lax.rsqrt(ms + EPS) * jnp.where(is_ctx, gain_c, gain_l) + jnp.where(is_ctx, shc, shl)
        h_ref[rows, :] = y.astype(BF16)
        return carry

    lax.fori_loop(0, tm // PACK, body, 0, unroll=3)


def _nmm_kernel(x_ref, nw_ref, shl_ref, scl_ref, shc_ref, scc_ref, w_ref, o_ref, h_ref, *, tm, t_lat):
    @pl.when(pl.program_id(2) == 0)
    def _():
        _norm_mod_tile(x_ref, h_ref, tm, pl.program_id(1) * tm, t_lat,
                       nw_ref[...], shl_ref[0], scl_ref[0], shc_ref[...], scc_ref[...])

    o_ref[0] = _dot(h_ref[...], w_ref[...]).astype(o_ref.dtype)


def _norm_mod_matmul(x, nw, shl, scl, shc, scc, w, *, t_lat, tm, tn):
    b, r, d = x.shape
    n = w.shape[1]
    vec_l = pl.BlockSpec((1, 1, d), lambda bb, i, j: (bb, 0, 0))
    vec_c = pl.BlockSpec((1, d), lambda bb, i, j: (0, 0))
    return pl.pallas_call(
        functools.partial(_nmm_kernel, tm=tm, t_lat=t_lat),
        grid=(b, r // tm, n // tn),
        in_specs=[
            pl.BlockSpec((1, tm, d), lambda bb, i, j: (bb, i, 0)),
            vec_c, vec_l, vec_l, vec_c, vec_c,
            pl.BlockSpec((d, tn), lambda bb, i, j: (0, j)),
        ],
        out_specs=pl.BlockSpec((1, tm, tn), lambda bb, i, j: (bb, i, j)),
        out_shape=jax.ShapeDtypeStruct((b, r, n), BF16),
        scratch_shapes=[pltpu.VMEM((tm, d), BF16)],
        compiler_params=_cp(("arbitrary", "arbitrary", "arbitrary")),
        name="norm_mod_matmul",
    )(x, nw, shl, scl, shc, scc, w)


def _halo_specs(tm, tc, col_of, r_tot, halo):
    nb = tm // halo
    last = r_tot // halo - 1
    prev = pl.BlockSpec((1, halo, tc), lambda b, i, j: (b, jnp.maximum(i * nb - 1, 0), col_of(j)))
    nxt = pl.BlockSpec((1, halo, tc), lambda b, i, j: (b, jnp.minimum((i + 1) * nb, last), col_of(j)))
    return prev, nxt


def _conv_silu_kernel(p_ref, m_ref, n_ref, w_ref, b_ref, o_ref, ext_ref, *, tm, ksize, t_lat, r_tot):
    main = m_ref[0].astype(F32)
    ext_ref[0:PACK, :] = p_ref[0].astype(F32)
    ext_ref[PACK:PACK + tm, :] = main
    ext_ref[PACK + tm:2 * PACK + tm, :] = n_ref[0].astype(F32)
    i = pl.program_id(1)
    w = w_ref[...]
    centre = main * w[ksize // 2:ksize // 2 + 1, :] + b_ref[...]
    taps = [(k, k - ksize // 2) for k in range(ksize) if k != ksize // 2]
    has_edge = (i == 0) | ((i + 1) * tm >= t_lat)

    @pl.when(has_edge)
    def _():
        t = i * tm + lax.broadcasted_iota(jnp.int32, (tm, 1), 0)
        is_ctx = t >= t_lat
        lo = jnp.where(is_ctx, t_lat, 0)
        hi = jnp.where(is_ctx, r_tot, t_lat)
        acc = centre
        for k, dk in taps:
            n = t + dk
            acc = acc + jnp.where((n >= lo) & (n < hi), ext_ref[pl.ds(PACK + dk, tm), :], 0.0) * w[k:k + 1, :]
        o_ref[0] = _silu(acc).astype(o_ref.dtype)

    @pl.when(jnp.logical_not(has_edge))
    def _():
        acc = centre
        for k, dk in taps:
            acc = acc + ext_ref[pl.ds(PACK + dk, tm), :] * w[k:k + 1, :]
        o_ref[0] = _silu(acc).astype(o_ref.dtype)


def _conv_silu(u, col0, width, w, bias, *, t_lat, tm, tc=512):
    b, r, _ = u.shape
    ksize = w.shape[0]
    cb = col0 // tc
    prev, nxt = _halo_specs(tm, tc, lambda j: cb + j, r, PACK)
    return pl.pallas_call(
        functools.partial(_conv_silu_kernel, tm=tm, ksize=ksize, t_lat=t_lat, r_tot=r),
        grid=(b, r // tm, width // tc),
        in_specs=[
            prev,
            pl.BlockSpec((1, tm, tc), lambda bb, i, j: (bb, i, cb + j)),
            nxt,
            pl.BlockSpec((ksize, tc), lambda bb, i, j: (0, j)),
            pl.BlockSpec((1, tc), lambda bb, i, j: (0, j)),
        ],
        out_specs=pl.BlockSpec((1, tm, tc), lambda bb, i, j: (bb, i, j)),
        out_shape=jax.ShapeDtypeStruct((b, r, width), BF16),
        scratch_shapes=[pltpu.VMEM((tm + 2 * PACK, tc), F32)],
        compiler_params=_cp(("arbitrary", "arbitrary", "arbitrary")),
        name="conv_silu",
    )(u, u, u, w, bias.reshape(1, width))


def _level_masks(c):
    t = np.arange(c)[:, None]
    s = np.arange(c)[None, :]
    ms = [(t == s)]
    h = 1
    while h < c:
        g = 2 * h
        ms.append((t // g == s // g) & ((t % g) >= h) & ((s % g) < h))
        h *= 2
    fwd = np.stack(ms).astype(np.float32)
    return np.stack([fwd, np.transpose(fwd, (0, 2, 1))])


def _hgrn_dir(q_raw, v, f_raw, lbp, masks_ref, d, st_ref, o_ref, rev, c, nheads, hd):
    width = nheads * hd
    llb, l1m, oml = lbp[0:1, :], lbp[1:2, :], lbp[2:3, :]
    q = _silu(q_raw)
    e = jnp.exp(-jnp.abs(f_raw))
    inv = 1.0 / (1.0 + e)
    k = oml * jnp.where(f_raw >= 0, e * inv, inv)
    a = l1m + jnp.minimum(f_raw, 0.0) - jnp.log(1.0 + e)
    logf = jnp.maximum(llb, a) + jnp.log(1.0 + jnp.exp(-jnp.abs(llb - a)))

    t_idx = lax.broadcasted_iota(jnp.int32, (c, 1), 0)
    p = logf
    tot = logf
    levels = []
    h = 1
    while h < SUBLANE:
        up = (t_idx & h) != 0
        tgt = jnp.logical_not(up) if rev else up
        levels.append(jnp.exp(jnp.where(tgt, p, tot - p)))
        sib = jnp.where(up, pltpu.roll(tot, h, 0), pltpu.roll(tot, c - h, 0))
        p = p + jnp.where(tgt, sib, 0.0)
        tot = tot + sib
        h *= 2
    nblk = c // SUBLANE
    pb = [p[b * SUBLANE:(b + 1) * SUBLANE] for b in range(nblk)]
    tb = [tot[b * SUBLANE:(b + 1) * SUBLANE] for b in range(nblk)]
    m = 1
    while m < nblk:
        tgt = [((b // m) % 2 == 1) != rev for b in range(nblk)]
        levels.append(jnp.exp(jnp.concatenate([pb[b] if tgt[b] else tb[b] - pb[b] for b in range(nblk)], axis=0)))
        pb = [pb[b] + tb[b ^ m] if tgt[b] else pb[b] for b in range(nblk)]
        pair = {}
        for b in range(nblk):
            if min(b, b ^ m) not in pair:
                pair[min(b, b ^ m)] = tb[b] + tb[b ^ m]
        tb = [pair[min(b, b ^ m)] for b in range(nblk)]
        m *= 2
    p = jnp.concatenate(pb, axis=0)
    eb = jnp.exp(p)
    ek = jnp.exp(jnp.concatenate([tb[b] - pb[b] for b in range(nblk)], axis=0))
    etot = jnp.exp(tb[0][0:1, :])

    for hh in range(nheads):
        sl = slice(hh * hd, (hh + 1) * hd)
        qh, kh, vh = q[:, sl], k[:, sl], v[:, sl]
        a_mat = _dot_nt(qh.astype(BF16), kh.astype(BF16)) * masks_ref[d, 0]
        for li, lev in enumerate(levels):
            eh = lev[:, sl]
            a_mat = a_mat + _dot_nt((qh * eh).astype(BF16), (kh * eh).astype(BF16)) * masks_ref[d, li + 1]
        st = st_ref[d, hh]
        o = _dot(a_mat.astype(BF16), vh.astype(BF16))
        o = o + _dot_nt((qh * eb[:, sl]).astype(BF16), st.astype(BF16))
        o_ref[0, :, sl] = o.astype(o_ref.dtype)
        khat = kh * ek[:, sl]
        st_ref[d, hh] = st * etot[:, sl] + _dot(vh.T.astype(BF16), khat.astype(BF16))


def _hgrn_kernel(qf_ref, if_ref, ff_ref, qb_ref, ib_ref, fb_ref, lbf_ref, lbb_ref, masks_ref,
                 of_ref, ob_ref, st_ref, *, c, nheads, hd):
    @pl.when(pl.program_id(1) == 0)
    def _():
        st_ref[...] = jnp.zeros_like(st_ref)

    ld = lambda ref: ref[0].astype(F32)
    _hgrn_dir(ld(qf_ref), ld(if_ref), ld(ff_ref), lbf_ref[...], masks_ref, 0, st_ref, of_ref, False, c, nheads, hd)
    _hgrn_dir(ld(qb_ref), ld(ib_ref), ld(fb_ref), lbb_ref[...], masks_ref, 1, st_ref, ob_ref, True, c, nheads, hd)


def _scan_chunk_maps(n_lat, n_ctx):
    n = n_lat + n_ctx
    fwd = lambda j: (j + n_lat) % n
    bwd = lambda j: n - 1 - j
    return n, fwd, bwd


def _hgrn_scan(u, cq, ci, cff, cfb, lbp_f, lbp_b, *, t_lat, c=CHUNK, nheads=A_HEADS, hd=A_DIM):
    b, r, _ = u.shape
    width = nheads * hd
    n, fwd, bwd = _scan_chunk_maps(t_lat // c, (r - t_lat) // c)
    masks = jnp.asarray(_level_masks(c))

    def spec(col, cm):
        return pl.BlockSpec((1, c, width), lambda bb, j: (bb, cm(j), col // width))

    const2 = pl.BlockSpec((3, width), lambda bb, j: (0, 0))
    out_shape = jax.ShapeDtypeStruct((b, r, width), BF16)
    return pl.pallas_call(
        functools.partial(_hgrn_kernel, c=c, nheads=nheads, hd=hd),
        grid=(b, n),
        in_specs=[spec(cq, fwd), spec(ci, fwd), spec(cff, fwd), spec(cq, bwd), spec(ci, bwd), spec(cfb, bwd),
                  const2, const2,
                  pl.BlockSpec(masks.shape, lambda bb, j: (0, 0, 0, 0))],
        out_specs=[pl.BlockSpec((1, c, width), lambda bb, j: (bb, fwd(j), 0)),
                   pl.BlockSpec((1, c, width), lambda bb, j: (bb, bwd(j), 0))],
        out_shape=[out_shape, out_shape],
        scratch_shapes=[pltpu.VMEM((2, nheads, hd, hd), F32)],
        compiler_params=_cp(("arbitrary", "arbitrary")),
        name="hgrn2_scan",
    )(u, u, u, u, u, u, lbp_f, lbp_b, masks)


def _ssd_dir(xm, bm, cm, dtc_raw, dtr_raw, prow, pcol, tri_ref, d, st_ref, o_ref, rev, c):
    nh = M_HEADS
    dt_c = _softplus(dtc_raw + prow[0:1, :])
    lf_c = dt_c * prow[1:2, :]
    dt_r = _softplus(dtr_raw + pcol[:, 0:1])[d * nh:(d + 1) * nh]
    lf_r = dt_r * pcol[d * nh:(d + 1) * nh, 1:2]
    lo, up = tri_ref[0], tri_ref[1]
    if rev:
        b_c = _dot_exact(up, lf_c)
        b_r = _dot_exact(lf_r, lo)
        causal = up > 0.5
        btot = b_c[0:1, :]
    else:
        b_c = _dot_exact(lo, lf_c)
        b_r = _dot_exact(lf_r, up)
        causal = lo > 0.5
        btot = b_c[c - 1:c, :]
    lane = lax.broadcasted_iota(jnp.int32, (1, LANE), 1)
    left = lane < M_HEADDIM
    bd_mask = jnp.concatenate([jnp.broadcast_to(left, (M_STATE, LANE)),
                               jnp.broadcast_to(jnp.logical_not(left), (M_STATE, LANE))], axis=0)
    for g in range(M_GROUPS):
        cg = cm[:, g * M_STATE:(g + 1) * M_STATE]
        bg_t = bm[:, g * M_STATE:(g + 1) * M_STATE].T
        gmat = _dot(cg.astype(BF16), bg_t.astype(BF16))
        for pp in range(nh // (2 * M_GROUPS)):
            pair = g * (nh // (2 * M_GROUPS)) + pp
            lhs, qs, kts, decs = [], [], [], []
            for h in (2 * pair, 2 * pair + 1):
                hc = d * nh + h
                bcol = b_c[:, hc:hc + 1]
                brow = b_r[h:h + 1, :]
                dtrow = dt_r[h:h + 1, :]
                dec = jnp.where(causal, jnp.exp(bcol - brow), 0.0) * dtrow
                lhs.append(gmat * dec)
                qs.append(cg * jnp.exp(bcol))
                bt = btot[:, hc:hc + 1]
                kts.append(bg_t * (jnp.exp(bt - brow) * dtrow))
                decs.append(jnp.broadcast_to(jnp.exp(bt), (M_STATE, LANE)))
            xp = xm[:, pair * LANE:(pair + 1) * LANE]
            rhs = jnp.concatenate([jnp.where(left, xp, 0.0), jnp.where(left, 0.0, xp)], axis=0)
            st = st_ref[d, pair]
            o = _dot(jnp.concatenate(lhs, axis=1).astype(BF16), rhs.astype(BF16))
            o = o + _dot(jnp.concatenate(qs, axis=1).astype(BF16), st.astype(BF16))
            o_ref[0, :, pair * LANE:(pair + 1) * LANE] = o.astype(o_ref.dtype)
            upd = _dot(jnp.concatenate(kts, axis=0).astype(BF16), xp.astype(BF16))
            st_ref[d, pair] = st * jnp.concatenate(decs, axis=0) + jnp.where(bd_mask, upd, 0.0)


def _ssd_kernel(xf_ref, bf_ref, cf_ref, dcf_ref, drf_ref, xb_ref, bb_ref, cb_ref, dcb_ref, drb_ref,
                prow_ref, pcol_ref, tri_ref, of_ref, ob_ref, st_ref, *, c):
    @pl.when(pl.program_id(1) == 0)
    def _():
        st_ref[...] = jnp.zeros_like(st_ref)

    ld = lambda ref: ref[0].astype(F32)
    _ssd_dir(ld(xf_ref), ld(bf_ref), ld(cf_ref), ld(dcf_ref), drf_ref[0], prow_ref[...], pcol_ref[...],
             tri_ref, 0, st_ref, of_ref, False, c)
    _ssd_dir(ld(xb_ref), ld(bb_ref), ld(cb_ref), ld(dcb_ref), drb_ref[0], prow_ref[...], pcol_ref[...],
             tri_ref, 1, st_ref, ob_ref, True, c)


def _tri_mats(c):
    t = np.arange(c)[:, None]
    s = np.arange(c)[None, :]
    return np.stack([(s <= t), (s >= t)]).astype(np.float32)


def _ssd_scan(xbc, u, dt_t, prow, pcol, *, t_lat, c=CHUNK):
    b, r, _ = xbc.shape
    n, fwd, bwd = _scan_chunk_maps(t_lat // c, (r - t_lat) // c)
    gs = M_GROUPS * M_STATE
    tri = jnp.asarray(_tri_mats(c))

    def specs(cm):
        return [pl.BlockSpec((1, c, M_INNER), lambda bb, j: (bb, cm(j), 0)),
                pl.BlockSpec((1, c, gs), lambda bb, j: (bb, cm(j), M_INNER // gs)),
                pl.BlockSpec((1, c, gs), lambda bb, j: (bb, cm(j), M_INNER // gs + 1)),
                pl.BlockSpec((1, c, LANE), lambda bb, j: (bb, cm(j), U_DT // LANE)),
                pl.BlockSpec((1, 2 * M_HEADS, c), lambda bb, j: (bb, 0, cm(j)))]

    out_shape = jax.ShapeDtypeStruct((b, r, M_INNER), BF16)
    return pl.pallas_call(
        functools.partial(_ssd_kernel, c=c),
        grid=(b, n),
        in_specs=specs(fwd) + specs(bwd) + [
            pl.BlockSpec(prow.shape, lambda bb, j: (0, 0)),
            pl.BlockSpec(pcol.shape, lambda bb, j: (0, 0)),
            pl.BlockSpec(tri.shape, lambda bb, j: (0, 0, 0))],
        out_specs=[pl.BlockSpec((1, c, M_INNER), lambda bb, j: (bb, fwd(j), 0)),
                   pl.BlockSpec((1, c, M_INNER), lambda bb, j: (bb, bwd(j), 0))],
        out_shape=[out_shape, out_shape],
        scratch_shapes=[pltpu.VMEM((2, M_HEADS // 2, 2 * M_STATE, LANE), F32)],
        compiler_params=_cp(("arbitrary", "arbitrary")),
        name="ssd_scan",
    )(xbc, xbc, xbc, u, dt_t, xbc, xbc, xbc, u, dt_t, prow, pcol, tri)


def _band_mats(windows, tb, period):
    t = np.arange(tb)[:, None]
    s = np.arange(tb)[None, :]
    out = []
    for w in windows:
        left = w // 2
        right = w - 1 - left
        out.append((t // period == s // period) & (s >= t - left) & (s <= t + right))
    return np.stack(out).astype(np.float32)


def _clip_count(idx, n, left, right):
    return jnp.minimum(idx + right + 1, n) - jnp.maximum(idx - left, 0)


def _pool_kernel(z_ref, band_ref, gw_ref, sc_ref, o_ref, pad_ref, *, n, tb, grid_w, windows):
    g = pl.program_id(1)
    band = band_ref[0].astype(BF16)
    shift = int(math.log2(grid_w)) if grid_w else 0
    for gi, w in enumerate(windows):
        @pl.when(g == gi)
        def _(w=w):
            left = w // 2
            right = w - 1 - left
            wr = w if grid_w else 1
            top = left * grid_w
            if grid_w:
                pad_ref[0:top, :] = jnp.zeros((top, LANE), F32)
                if right:
                    pad_ref[top + n:top + n + right * grid_w, :] = jnp.zeros((right * grid_w, LANE), F32)

            def col_sum(i, carry):
                pad_ref[pl.ds(top + i * tb, tb), :] = _dot(band, z_ref[0, pl.ds(i * tb, tb), :].astype(BF16))
                return carry

            unroll = min(4, n // tb)
            lax.fori_loop(0, n // tb, col_sum, 0, unroll=unroll)

            def finish(i, carry):
                acc = pad_ref[pl.ds(i * tb, tb), :]
                for kk in range(1, wr):
                    acc = acc + pad_ref[pl.ds(i * tb + kk * grid_w, tb), :]
                t = i * tb + lax.broadcasted_iota(jnp.int32, (tb, 1), 0)
                if grid_w:
                    cnt = (_clip_count(lax.shift_right_logical(t, shift), n // grid_w, left, right)
                           * _clip_count(t & (grid_w - 1), grid_w, left, right))
                else:
                    cnt = _clip_count(t, n, left, right)
                y = acc / cnt.astype(F32) - z_ref[0, pl.ds(i * tb, tb), :].astype(F32)
                out = _dot(y.astype(BF16), gw_ref[0].astype(BF16)) * sc_ref[0]
                o_ref[0, pl.ds(i * tb, tb), :] = out.astype(o_ref.dtype)
                return carry

            lax.fori_loop(0, n // tb, finish, 0, unroll=unroll)


def _pool(u, group_w, scale, *, row0, n, grid_w):
    b = u.shape[0]
    tb = 2 * grid_w if grid_w else n
    bands = jnp.asarray(_band_mats(P_WINDOWS, tb, grid_w if grid_w else n))
    rb = row0 // n
    pad_rows = n + (max(P_WINDOWS) - 1) * grid_w
    return pl.pallas_call(
        functools.partial(_pool_kernel, n=n, tb=tb, grid_w=grid_w, windows=P_WINDOWS),
        grid=(b, len(P_WINDOWS)),
        in_specs=[
            pl.BlockSpec((1, n, P_GROUP), lambda bb, g: (bb, rb, U_P // P_GROUP + g)),
            pl.BlockSpec((1, tb, tb), lambda bb, g: (g, 0, 0)),
            pl.BlockSpec((1, P_GROUP, P_GROUP), lambda bb, g: (g, 0, 0)),
            pl.BlockSpec((1, 1, P_GROUP), lambda bb, g: (g, 0, 0)),
        ],
        out_specs=pl.BlockSpec((1, n, P_GROUP), lambda bb, g: (bb, 0, g)),
        out_shape=jax.ShapeDtypeStruct((b, n, P_WIDTH), BF16),
        scratch_shapes=[pltpu.VMEM((pad_rows, LANE), F32)],
        compiler_params=_cp(("arbitrary", "arbitrary")),
        name="pool_grid" if grid_w else "pool_seq",
    )(u, bands, group_w, scale.reshape(len(P_WINDOWS), 1, P_GROUP))


def _chan_dft_mats():
    k = np.arange(F_WIDTH // 4)
    ang = 2.0 * np.pi * ((k[:, None] * k[None, :]) % len(k)) / len(k)
    eye = np.eye(4)
    w = np.concatenate([np.kron(eye, np.cos(ang)), np.kron(eye, np.sin(ang))], axis=1)
    hi = w.astype(np.float32).astype(BF16)
    lo = (w - hi.astype(np.float64)).astype(np.float32).astype(BF16)
    return jnp.asarray(hi), jnp.asarray(lo)


def _chan_dft_kernel(z_ref, wh_ref, wl_ref, o_ref):
    z = z_ref[0]
    o_ref[0] = (_dot(z, wh_ref[...]) + _dot(z, wl_ref[...])).astype(o_ref.dtype)


def _chan_dft(u, *, tm):
    b, r, _ = u.shape
    wh, wl = _chan_dft_mats()
    wspec = pl.BlockSpec(wh.shape, lambda bb, i: (0, 0))
    return pl.pallas_call(
        _chan_dft_kernel,
        grid=(b, r // tm),
        in_specs=[pl.BlockSpec((1, tm, F_WIDTH), lambda bb, i: (bb, i, U_F // F_WIDTH)), wspec, wspec],
        out_specs=pl.BlockSpec((1, tm, 2 * F_WIDTH), lambda bb, i: (bb, i, 0)),
        out_shape=jax.ShapeDtypeStruct((b, r, 2 * F_WIDTH), BF16),
        compiler_params=_cp(("arbitrary", "arbitrary")),
        name="chan_dft",
    )(u, wh, wl)


def _dft_cols(n, ncols):
    k = jnp.arange(n, dtype=jnp.int32)[:, None]
    j = jnp.arange(ncols, dtype=jnp.int32)[None, :]
    ang = ((k * j) & (n - 1)).astype(F32) * (2.0 * math.pi / n)
    return jnp.cos(ang), jnp.sin(ang)


ROT_PERIOD = 16


def _rot_patterns(n, tk):
    assert ROT_PERIOD % (n // tk) == 0
    k = np.arange(ROT_PERIOD)[None, :, None]
    t0 = (np.arange(n // tk) * tk)[:, None, None]
    ang = 2.0 * np.pi * ((k * t0) % n) / n
    shape = (n // tk, ROT_PERIOD, tk)
    return (jnp.asarray(np.broadcast_to(np.cos(ang), shape), F32), jnp.asarray(np.broadcast_to(np.sin(ang), shape), F32))


def _fold_mats(tf):
    flip = np.zeros((tf, tf), np.float32)
    flip[np.arange(1, tf), tf - np.arange(1, tf)] = 1.0
    first = np.zeros((tf, PACK), np.float32)
    first[0, 0] = 1.0
    return jnp.asarray(flip, BF16), jnp.asarray(first, BF16)


def _dft_fold_kernel(cur_ref, mir_ref, nxt_ref, flip_ref, first_ref, o_ref):
    tf = cur_ref.shape[1]
    mirrored = _dot(flip_ref[...], mir_ref[0]) + _dot(first_ref[...], nxt_ref[0])
    row = pl.program_id(1) * tf + lax.broadcasted_iota(jnp.int32, (tf, 1), 0)
    mirrored = jnp.where(row == 0, 0.0, mirrored)
    cur = cur_ref[0].astype(F32)
    sign = jnp.where(lax.broadcasted_iota(jnp.int32, (1, 2 * F_WIDTH), 1) < F_WIDTH, 1.0, -1.0)
    o_ref[0] = (cur + sign * mirrored).astype(o_ref.dtype)


def _dft_fold(zcs, *, n, tf=512):
    b, r, w = zcs.shape
    nt = n // tf
    flip, first = _fold_mats(tf)
    last16 = r // PACK - 1
    return pl.pallas_call(
        _dft_fold_kernel,
        grid=(b, nt // 2),
        in_specs=[pl.BlockSpec((1, tf, w), lambda bb, i: (bb, i, 0)),
                  pl.BlockSpec((1, tf, w), lambda bb, i: (bb, nt - 1 - i, 0)),
                  pl.BlockSpec((1, PACK, w), lambda bb, i: (bb, jnp.minimum((nt - i) * (tf // PACK), last16), 0)),
                  pl.BlockSpec(flip.shape, lambda bb, i: (0, 0)),
                  pl.BlockSpec(first.shape, lambda bb, i: (0, 0))],
        out_specs=pl.BlockSpec((1, tf, w), lambda bb, i: (bb, i, 0)),
        out_shape=jax.ShapeDtypeStruct((b, n // 2, w), BF16),
        compiler_params=_cp(("arbitrary", "arbitrary")),
        name="dft_fold",
    )(zcs, zcs, zcs, flip, first)


def _time_dft_kernel(cb_ref, sb_ref, pc_ref, ps_ref, zc_ref, zs_ref, zh_ref, o_ref, acc_ref,
                     *, nb, tm, tk, scale, folded):
    kk = pl.program_id(1)

    @pl.when(kk == 0)
    def _():
        acc_ref[...] = jnp.zeros_like(acc_ref)

    cb = cb_ref[...].reshape(tm // ROT_PERIOD, ROT_PERIOD, tk)
    sb = sb_ref[...].reshape(tm // ROT_PERIOD, ROT_PERIOD, tk)
    ca, sa = pc_ref[...], ps_ref[...]
    ct = (ca * cb - sa * sb).reshape(tm, tk).astype(BF16)
    st = (sa * cb + ca * sb).reshape(tm, tk).astype(BF16)
    for bb in range(nb):
        acc_ref[bb] += _dot(ct, zc_ref[bb]) - _dot(st, zs_ref[bb])

    @pl.when(kk == pl.num_programs(1) - 1)
    def _():
        y = acc_ref[...]
        if folded:
            row = lax.broadcasted_iota(jnp.int32, (1, tm, 1), 1)
            y = y + jnp.where((row & 1) == 0, 1.0, -1.0) * zh_ref[:, 0:1, :].astype(F32)
        o_ref[...] = (y * scale).astype(o_ref.dtype)


def _time_dft(zcs, *, row0, n, tm, tk, folded):
    b = zcs.shape[0]
    scale = 1.0 / math.sqrt(n * (F_WIDTH // 4))
    cb, sb = _dft_cols(n, tk)
    pc, ps = _rot_patterns(n, tk)
    pat = pl.BlockSpec((None, ROT_PERIOD, tk), lambda i, k: (k, 0, 0))
    z = _dft_fold(zcs, n=n) if folded else zcs
    zrow = 0 if folded else row0 // tk
    return pl.pallas_call(
        functools.partial(_time_dft_kernel, nb=b, tm=tm, tk=tk, scale=scale, folded=folded),
        grid=(n // tm, (n // 2 if folded else n) // tk),
        in_specs=[
            pl.BlockSpec((tm, tk), lambda i, k: (i, 0)),
            pl.BlockSpec((tm, tk), lambda i, k: (i, 0)),
            pat, pat,
            pl.BlockSpec((b, tk, F_WIDTH), lambda i, k: (0, zrow + k, 0)),
            pl.BlockSpec((b, tk, F_WIDTH), lambda i, k: (0, zrow + k, 1)),
            pl.BlockSpec((b, PACK, F_WIDTH), lambda i, k: (0, (row0 + n // 2) // PACK, 0)),
        ],
        out_specs=pl.BlockSpec((b, tm, F_WIDTH), lambda i, k: (0, i, 0)),
        out_shape=jax.ShapeDtypeStruct((b, n, F_WIDTH), BF16),
        scratch_shapes=[pltpu.VMEM((b, tm, F_WIDTH), F32)],
        compiler_params=_cp(("arbitrary", "arbitrary")),
        name="time_dft_%d" % n,
    )(cb, sb, pc, ps, z, z, zcs)


def _merge_kernel(oaf_ref, oab_ref, g_ref, omf_ref, omb_ref, xm_ref, z0_ref, z1_ref, pm_ref, fm_ref, pmc_ref, fmc_ref,
                  anw_ref, dsk_ref, mnw_ref, wbr_ref, g0_ref, g1_ref, g2_ref, g3_ref, o_ref, act_ref):
    @pl.when(pl.program_id(2) == 0)
    def _():
        ld = lambda ref: ref[0].astype(F32)
        oa = ld(oaf_ref) + ld(oab_ref)
        gate = _silu(ld(g_ref))
        for hh in range(A_HEADS):
            sl = slice(hh * A_DIM, (hh + 1) * A_DIM)
            oh = oa[:, sl]
            ms = jnp.mean(oh * oh, axis=-1, keepdims=True)
            act_ref[:, sl] = (oh * lax.rsqrt(ms + EPS) * anw_ref[...] * gate[:, sl]).astype(BF16)
        y = ld(omf_ref) + ld(omb_ref) + dsk_ref[...] * ld(xm_ref)
        z = jnp.concatenate([ld(z0_ref), ld(z1_ref)], axis=1)
        y = y * _silu(z)
        ms = jnp.mean(y * y, axis=-1, keepdims=True)
        act_ref[:, A_WIDTH:A_WIDTH + M_INNER] = (y * lax.rsqrt(ms + EPS) * mnw_ref[...]).astype(BF16)
        act_ref[:, A_WIDTH + M_INNER:A_WIDTH + M_INNER + P_WIDTH] = pm_ref[0].astype(BF16)
        act_ref[:, A_WIDTH + M_INNER + P_WIDTH:] = fm_ref[0].astype(BF16)

        @pl.when(pl.program_id(1) == pl.num_programs(1) - 1)
        def _():
            t_ctx = pmc_ref.shape[1]
            tail = slice(act_ref.shape[0] - t_ctx, act_ref.shape[0])
            act_ref[tail, A_WIDTH + M_INNER:A_WIDTH + M_INNER + P_WIDTH] = pmc_ref[0].astype(BF16)
            act_ref[tail, A_WIDTH + M_INNER + P_WIDTH:] = fmc_ref[0].astype(BF16)

    offs = (0, A_WIDTH, A_WIDTH + M_INNER, A_WIDTH + M_INNER + P_WIDTH, ACT_W)
    acc = None
    for br, gr in enumerate((g0_ref, g1_ref, g2_ref, g3_ref)):
        yb = _dot(act_ref[:, offs[br]:offs[br + 1]], wbr_ref[offs[br]:offs[br + 1], :])
        term = _sigmoid(gr[0].astype(F32)) * yb
        acc = term if acc is None else acc + term
    o_ref[0] = acc.astype(o_ref.dtype)


def _merge(u, oaf, oab, omf, omb, xbc, pm, fm, pmc, fmc, anw, dsk, mnw, wbr, *, tm, tn=512):
    b, r, _ = u.shape
    t_ctx = pmc.shape[1]
    assert r - pm.shape[1] == t_ctx and t_ctx <= tm and r % tm == 0
    ctx_rows = pl.BlockSpec((1, t_ctx, P_WIDTH), lambda bb, i, j: (bb, 0, 0))

    def rows(width, cb):
        return pl.BlockSpec((1, tm, width), lambda bb, i, j: (bb, i, cb))

    def vec(width):
        return pl.BlockSpec((1, width), lambda bb, i, j: (0, 0))

    def gate(br):
        return pl.BlockSpec((1, tm, tn), lambda bb, i, j: (bb, i, (U_GATE + br * D_MODEL) // tn + j))

    return pl.pallas_call(
        _merge_kernel,
        grid=(b, r // tm, D_MODEL // tn),
        in_specs=[rows(A_WIDTH, 0), rows(A_WIDTH, 0), rows(A_WIDTH, U_G // A_WIDTH),
                  rows(M_INNER, 0), rows(M_INNER, 0), rows(M_INNER, 0),
                  rows(512, U_Z // 512), rows(512, U_Z // 512 + 1),
                  rows(P_WIDTH, 0), rows(F_WIDTH, 0), ctx_rows, ctx_rows,
                  vec(A_DIM), vec(M_INNER), vec(M_INNER),
                  pl.BlockSpec((ACT_W, tn), lambda bb, i, j: (0, j)),
                  gate(0), gate(1), gate(2), gate(3)],
        out_specs=pl.BlockSpec((1, tm, tn), lambda bb, i, j: (bb, i, j)),
        out_shape=jax.ShapeDtypeStruct((b, r, D_MODEL), BF16),
        scratch_shapes=[pltpu.VMEM((tm, ACT_W), BF16)],
        compiler_params=_cp(("arbitrary", "arbitrary", "arbitrary")),
        name="merge",
    )(oaf, oab, u, omf, omb, xbc, u, u, pm, fm, pmc, fmc, anw, dsk, mnw, wbr, u, u, u, u)


def _proj_resid_kernel(y_ref, w_ref, x_ref, gl_ref, gc_ref, o_ref, *, tm, t_lat):
    is_ctx = _row_is_ctx(pl.program_id(1) * tm, tm, t_lat)
    gate = jnp.where(is_ctx, gc_ref[...], gl_ref[0])
    o_ref[0] = x_ref[0] + gate * _dot(y_ref[0], w_ref[...])


def _proj_resid(y, w, x, gl, gc, *, t_lat, tm):
    b, r, d = x.shape
    k = y.shape[2]
    return pl.pallas_call(
        functools.partial(_proj_resid_kernel, tm=tm, t_lat=t_lat),
        grid=(b, r // tm),
        in_specs=[pl.BlockSpec((1, tm, k), lambda bb, i: (bb, i, 0)),
                  pl.BlockSpec((k, d), lambda bb, i: (0, 0)),
                  pl.BlockSpec((1, tm, d), lambda bb, i: (bb, i, 0)),
                  pl.BlockSpec((1, 1, d), lambda bb, i: (bb, 0, 0)),
                  pl.BlockSpec((1, d), lambda bb, i: (0, 0))],
        out_specs=pl.BlockSpec((1, tm, d), lambda bb, i: (bb, i, 0)),
        out_shape=jax.ShapeDtypeStruct((b, r, d), F32),
        compiler_params=_cp(("arbitrary", "arbitrary")),
        name="proj_resid",
    )(y, w, x, gl, gc)


def _conv3(ext, s_ref, w, bias, tm, m_prev, m_next):
    main = ext[0:tm]
    s_ref[SUBLANE:SUBLANE + tm, :] = main
    s_ref[SUBLANE - 1:SUBLANE, :] = ext[tm + SUBLANE - 1:tm + SUBLANE]
    s_ref[SUBLANE + tm:SUBLANE + tm + 1, :] = ext[tm + SUBLANE:tm + SUBLANE + 1]
    prev = s_ref[pl.ds(SUBLANE - 1, tm), :]
    nxt = s_ref[pl.ds(SUBLANE + 1, tm), :]
    return main * w[1:2, :] + (prev * m_prev) * w[0:1, :] + (nxt * m_next) * w[2:3, :] + bias


def _ffn_kernel(xp_ref, x_ref, xn_ref, nw_ref, shl_ref, scl_ref, shc_ref, scc_ref, gl_ref, gc_ref, fnw_ref,
                wa_ref, wb_ref, cwa_ref, cwb_ref, ba_ref, bb_ref, wd_ref, o_ref, h_ref, sa_ref, sb_ref,
                *, tm, t_lat, r_tot, final_norm):
    i = pl.program_id(1)
    kk = pl.program_id(2)
    row0 = i * tm

    @pl.when(kk == 0)
    def _():
        vecs = (nw_ref[...], shl_ref[0], scl_ref[0], shc_ref[...], scc_ref[...])
        _norm_mod_tile(x_ref, h_ref, tm, row0, t_lat, *vecs)
        halo = jnp.concatenate([xp_ref[0], xn_ref[0]], axis=0)
        hrow = lax.broadcasted_iota(jnp.int32, (2 * SUBLANE, 1), 0)
        grow = jnp.where(hrow < SUBLANE, row0 - SUBLANE + hrow, row0 + tm - SUBLANE + hrow)
        h_ref[tm:tm + 2 * SUBLANE, :] = _norm_mod(halo, grow >= t_lat, *vecs).astype(BF16)
        o_ref[...] = jnp.zeros_like(o_ref)

    t = row0 + lax.broadcasted_iota(jnp.int32, (tm, 1), 0)
    m_prev = jnp.where((t == 0) | (t == t_lat), 0.0, 1.0)
    m_next = jnp.where((t == t_lat - 1) | (t == r_tot - 1), 0.0, 1.0)
    h = h_ref[...]
    a = _conv3(_dot(h, wa_ref[...]), sa_ref, cwa_ref[...], ba_ref[...], tm, m_prev, m_next)
    g = _conv3(_dot(h, wb_ref[...]), sb_ref, cwb_ref[...], bb_ref[...], tm, m_prev, m_next)
    o_ref[0] += _dot((_silu(a) * g).astype(BF16), wd_ref[...])

    @pl.when(kk == pl.num_programs(2) - 1)
    def _():
        gate = jnp.where(_row_is_ctx(row0, tm, t_lat), gc_ref[...], gl_ref[0])
        y = x_ref[0] + gate * o_ref[0]
        if final_norm:
            ms = jnp.mean(y * y, axis=-1, keepdims=True)
            y = y * lax.rsqrt(ms + EPS) * fnw_ref[...]
        o_ref[0] = y


def _ffn(x, nw, shl, scl, shc, scc, gl, gc, fnw, w_up, cw, cb, wd, *, t_lat, tm, tk, final_norm, out_rows):
    b, r, d = x.shape
    nk = D_FF // tk
    x_prev, x_next = _halo_specs(tm, d, lambda k: 0, r, SUBLANE)
    vec_l = pl.BlockSpec((1, 1, d), lambda bb, i, k: (bb, 0, 0))
    vec_c = pl.BlockSpec((1, d), lambda bb, i, k: (0, 0))
    cb2 = cb.reshape(1, 2 * D_FF)
    return pl.pallas_call(
        functools.partial(_ffn_kernel, tm=tm, t_lat=t_lat, r_tot=r, final_norm=final_norm),
        grid=(b, r // tm, nk),
        in_specs=[x_prev, pl.BlockSpec((1, tm, d), lambda bb, i, k: (bb, i, 0)), x_next,
                  vec_c, vec_l, vec_l, vec_c, vec_c, vec_l, vec_c, vec_c,
                  pl.BlockSpec((d, tk), lambda bb, i, k: (0, k)),
                  pl.BlockSpec((d, tk), lambda bb, i, k: (0, nk + k)),
                  pl.BlockSpec((FFN_CONV, tk), lambda bb, i, k: (0, k)),
                  pl.BlockSpec((FFN_CONV, tk), lambda bb, i, k: (0, nk + k)),
                  pl.BlockSpec((1, tk), lambda bb, i, k: (0, k)),
                  pl.BlockSpec((1, tk), lambda bb, i, k: (0, nk + k)),
                  pl.BlockSpec((tk, d), lambda bb, i, k: (k, 0))],
        out_specs=pl.BlockSpec((1, tm, d), lambda bb, i, k: (bb, i, 0)),
        out_shape=jax.ShapeDtypeStruct((b, out_rows, d), F32),
        scratch_shapes=[pltpu.VMEM((tm + 2 * SUBLANE, d), BF16),
                        pltpu.VMEM((tm + 2 * SUBLANE, tk), F32),
                        pltpu.VMEM((tm + 2 * SUBLANE, tk), F32)],
        compiler_params=_cp(("arbitrary", "arbitrary", "arbitrary")),
        name="ffn",
    )(x, x, x, nw, shl, scl, shc, scc, gl, gc, fnw, w_up, w_up, cw, cw, cb2, cb2, wd)


def _arrange_w_in(w):
    pad = jnp.zeros((w.shape[0], NU - W_END), w.dtype)
    return jnp.concatenate([w[:, W_GATE:W_END], w[:, W_A:W_DT], w[:, W_P:W_GATE], w[:, W_DT:W_P], pad],
                           axis=1).astype(BF16)


def kernel(x, c, ctx, c_ctx, w_ada, b_ada, norm1_w, w_in, a_lb_logits, a_norm_w, w_br_a, m_conv_w, m_conv_b,
           m_dt_bias, m_a_log, m_d, m_norm_w, w_br_m, p_group_w, p_scale, w_br_p, w_br_f, w_out, norm2_w,
           w_up, ffn_conv_w, ffn_conv_b, w_down, final_norm_w):
    nb, t_lat, d = x.shape
    t_ctx = ctx.shape[1]
    depth = w_ada.shape[0]
    tm_in, tm = 1056, 768
    assert t_lat % CHUNK == 0 and t_ctx % CHUNK == 0 and (t_lat + t_ctx) % tm_in == 0 and (t_lat + t_ctx) % tm == 0

    lb_all = jnp.cumsum(jax.nn.softmax(a_lb_logits.astype(F32), axis=0), axis=0)
    lb_all = lb_all - lb_all[0]

    xs = jnp.concatenate([x, ctx], axis=1)
    c_all = jnp.concatenate([c, c_ctx[None, :], jnp.zeros((SUBLANE - nb - 1, d), F32)], axis=0)
    mod = _ada(c_all, w_ada, b_ada)

    for l in range(depth):
        last = l == depth - 1
        ml = mod[l, :nb].reshape(nb, 1, 6, d)
        mc = mod[l, nb].reshape(6, d)
        lat = lambda k: ml[:, :, k, :]
        cx = lambda k: mc[k:k + 1, :]

        u = _norm_mod_matmul(xs, norm1_w[l][None, :], lat(0), lat(1), cx(0), cx(1), _arrange_w_in(w_in[l]),
                             t_lat=t_lat, tm=tm_in, tn=2432)
        lbp = [jnp.stack([jnp.log(lb_all[l, dd]), jnp.log1p(-lb_all[l, dd]), 1.0 - lb_all[l, dd]])
               for dd in range(2)]
        oaf, oab = _hgrn_scan(u, U_Q, U_I, U_FF, U_FB, lbp[0], lbp[1], t_lat=t_lat)

        xbc = _conv_silu(u, U_XBC, M_XBC, m_conv_w[l], m_conv_b[l], t_lat=t_lat, tm=tm)
        dt_t = jnp.transpose(u[:, :, U_DT:U_DT + 2 * M_HEADS].astype(F32), (0, 2, 1))
        a_neg = -jnp.exp(m_a_log[l].astype(F32)).reshape(-1)
        dtb = m_dt_bias[l].astype(F32).reshape(-1)
        prow = jnp.zeros((2, LANE), F32).at[0, :2 * M_HEADS].set(dtb).at[1, :2 * M_HEADS].set(a_neg)
        pcol = jnp.stack([dtb, a_neg], axis=1)
        omf, omb = _ssd_scan(xbc, u, dt_t, prow, pcol, t_lat=t_lat)

        pm = _pool(u, p_group_w[l], p_scale[l], row0=0, n=t_lat, grid_w=GRID_W)
        pmc = _pool(u, p_group_w[l], p_scale[l], row0=t_lat, n=t_ctx, grid_w=0)
        zcs = _chan_dft(u, tm=tm)
        fm = _time_dft(zcs, row0=0, n=t_lat, tm=1024, tk=512, folded=True)
        fmc = _time_dft(zcs, row0=t_lat, n=t_ctx, tm=t_ctx, tk=t_ctx, folded=False)

        wbr = jnp.concatenate([w_br_a[l], w_br_m[l], w_br_p[l], w_br_f[l]], axis=0).astype(BF16)
        dsk = jnp.repeat(m_d[l].astype(F32), M_HEADDIM)[None, :]
        anw = a_norm_w[l].astype(F32)[None, :]
        y = _merge(u, oaf, oab, omf, omb, xbc, pm, fm, pmc, fmc, anw, dsk, m_norm_w[l][None, :], wbr, tm=tm)
        xs = _proj_resid(y, w_out[l].astype(BF16), xs, lat(2), cx(2), t_lat=t_lat, tm=tm)

        xs = _ffn(xs, norm2_w[l][None, :], lat(3), lat(4), cx(3), cx(4), lat(5), cx(5), final_norm_w[None, :],
                  w_up[l].astype(BF16), ffn_conv_w[l], ffn_conv_b[l], w_down[l].astype(BF16),
                  t_lat=t_lat, tm=tm, tk=512, final_norm=last, out_rows=t_lat if last else t_lat + t_ctx)
    return xs
```

```python
import functools
import math

import numpy as np
import jax
import jax.numpy as jnp
from jax import lax
from jax.experimental import pallas as pl
from jax.experimental.pallas import tpu as pltpu

F32 = jnp.float32
BF16 = jnp.bfloat16

D_MODEL = 2048
GRID_W = 64
EPS = 1e-6

A_HEADS = 4
A_DIM = 128
A_WIDTH = A_HEADS * A_DIM
M_HEADS = 16
M_HEADDIM = 64
M_INNER = M_HEADS * M_HEADDIM
M_GROUPS = 2
M_STATE = 128
M_XBC = M_INNER + 2 * M_GROUPS * M_STATE
P_WINDOWS = (2, 4, 8, 16)
P_GROUP = 128
P_WIDTH = 512
F_WIDTH = 512
N_BRANCH = 4
D_FF = 5632
FFN_CONV = 3

W_A, W_Z, W_XBC, W_DT, W_P, W_F, W_GATE = 0, 2560, 3584, 5120, 5152, 5664, 6176
W_END = W_GATE + N_BRANCH * D_MODEL

U_GATE = 0
U_Q, U_I, U_FF, U_FB, U_G = 8192, 8704, 9216, 9728, 10240
U_Z = 10752
U_XBC = 11776
U_P = 13312
U_F = 13824
U_DT = 14336
NU = 14592
ACT_W = A_WIDTH + M_INNER + P_WIDTH + F_WIDTH

LANE = 128
SUBLANE = 8
PACK = 16
CHUNK = 128
VMEM_LIMIT = 56 * 1024 * 1024


def _cp(sem):
    return pltpu.CompilerParams(dimension_semantics=sem, vmem_limit_bytes=VMEM_LIMIT)


def _sigmoid(x):
    return 0.5 * jnp.tanh(0.5 * x) + 0.5


def _silu(x):
    return x * _sigmoid(x)


def _softplus(x):
    return jnp.maximum(x, 0.0) + jnp.log(1.0 + jnp.exp(-jnp.abs(x)))


def _dot(a, b):
    return jnp.dot(a, b, preferred_element_type=F32)


def _dot_nt(a, b):
    return lax.dot_general(a, b, (((1,), (1,)), ((), ())), preferred_element_type=F32)


def _dot_exact(a, b):
    return jnp.dot(a, b, preferred_element_type=F32, precision=lax.Precision.HIGHEST)


def _row_is_ctx(row0, tm, t_lat):
    row = row0 + lax.broadcasted_iota(jnp.int32, (tm, 1), 0)
    return row >= t_lat


def _ada_kernel(c_ref, w_ref, b_ref, o_ref):
    s = _silu(c_ref[...])
    o_ref[0] = _dot(s.astype(BF16), w_ref[0].astype(BF16)) + b_ref[0]


def _ada(c_all, w_ada, b_ada, tn=2048):
    nl, d, n6 = w_ada.shape
    return pl.pallas_call(
        _ada_kernel,
        grid=(nl, n6 // tn),
        in_specs=[
            pl.BlockSpec((SUBLANE, d), lambda l, j: (0, 0)),
            pl.BlockSpec((1, d, tn), lambda l, j: (l, 0, j)),
            pl.BlockSpec((1, 1, tn), lambda l, j: (l, 0, j)),
        ],
        out_specs=pl.BlockSpec((1, SUBLANE, tn), lambda l, j: (l, 0, j)),
        out_shape=jax.ShapeDtypeStruct((nl, SUBLANE, n6), F32),
        compiler_params=_cp(("arbitrary", "arbitrary")),
        name="ada",
    )(c_all, w_ada, b_ada.reshape(nl, 1, n6))


def _norm_mod(x, is_ctx, nw, shl, scl, shc, scc):
    ms = jnp.mean(x * x, axis=-1, keepdims=True)
    y = x * lax.rsqrt(ms + EPS) * nw
    return y * (1.0 + jnp.where(is_ctx, scc, scl)) + jnp.where(is_ctx, shc, shl)


def _norm_mod_tile(x_ref, h_ref, tm, row0, t_lat, nw, shl, scl, shc, scc):
    gain_l, gain_c = nw * (1.0 + scl), nw * (1.0 + scc)

    def body(r, carry):
        rows = pl.ds(pl.multiple_of(r * PACK, PACK), PACK)
        is_ctx = row0 + r * PACK >= t_lat
        x = x_ref[0, rows, :]
        ms = jnp.mean(x * x, axis=-1, keepdims=True)
        y = x * lax.rsqrt(ms + EPS) * jnp.where(is_ctx, gain_c, gain_l) + jnp.where(is_ctx, shc, shl)
        h_ref[rows, :] = y.astype(BF16)
        return carry

    lax.fori_loop(0, tm // PACK, body, 0, unroll=3)


def _nmm_kernel(x_ref, nw_ref, shl_ref, scl_ref, shc_ref, scc_ref, w_ref, o_ref, h_ref, *, tm, t_lat):
    @pl.when(pl.program_id(2) == 0)
    def _():
        _norm_mod_tile(x_ref, h_ref, tm, pl.program_id(1) * tm, t_lat,
                       nw_ref[...], shl_ref[0], scl_ref[0], shc_ref[...], scc_ref[...])

    o_ref[0] = _dot(h_ref[...], w_ref[...]).astype(o_ref.dtype)


def _norm_mod_matmul(x, nw, shl, scl, shc, scc, w, *, t_lat, tm, tn):
    b, r, d = x.shape
    n = w.shape[1]
    vec_l = pl.BlockSpec((1, 1, d), lambda bb, i, j: (bb, 0, 0))
    vec_c = pl.BlockSpec((1, d), lambda bb, i, j: (0, 0))
    return pl.pallas_call(
        functools.partial(_nmm_kernel, tm=tm, t_lat=t_lat),
        grid=(b, r // tm, n // tn),
        in_specs=[
            pl.BlockSpec((1, tm, d), lambda bb, i, j: (bb, i, 0)),
            vec_c, vec_l, vec_l, vec_c, vec_c,
            pl.BlockSpec((d, tn), lambda bb, i, j: (0, j)),
        ],
        out_specs=pl.BlockSpec((1, tm, tn), lambda bb, i, j: (bb, i, j)),
        out_shape=jax.ShapeDtypeStruct((b, r, n), BF16),
        scratch_shapes=[pltpu.VMEM((tm, d), BF16)],
        compiler_params=_cp(("arbitrary", "arbitrary", "arbitrary")),
        name="norm_mod_matmul",
    )(x, nw, shl, scl, shc, scc, w)


def _halo_specs(tm, tc, col_of, r_tot, halo):
    nb = tm // halo
    last = r_tot // halo - 1
    prev = pl.BlockSpec((1, halo, tc), lambda b, i, j: (b, jnp.maximum(i * nb - 1, 0), col_of(j)))
    nxt = pl.BlockSpec((1, halo, tc), lambda b, i, j: (b, jnp.minimum((i + 1) * nb, last), col_of(j)))
    return prev, nxt


def _conv_silu_kernel(p_ref, m_ref, n_ref, w_ref, b_ref, o_ref, ext_ref, *, tm, ksize, t_lat, r_tot):
    main = m_ref[0].astype(F32)
    ext_ref[0:PACK, :] = p_ref[0].astype(F32)
    ext_ref[PACK:PACK + tm, :] = main
    ext_ref[PACK + tm:2 * PACK + tm, :] = n_ref[0].astype(F32)
    i = pl.program_id(1)
    w = w_ref[...]
    centre = main * w[ksize // 2:ksize // 2 + 1, :] + b_ref[...]
    taps = [(k, k - ksize // 2) for k in range(ksize) if k != ksize // 2]
    has_edge = (i == 0) | ((i + 1) * tm >= t_lat)

    @pl.when(has_edge)
    def _():
        t = i * tm + lax.broadcasted_iota(jnp.int32, (tm, 1), 0)
        is_ctx = t >= t_lat
        lo = jnp.where(is_ctx, t_lat, 0)
        hi = jnp.where(is_ctx, r_tot, t_lat)
        acc = centre
        for k, dk in taps:
            n = t + dk
            acc = acc + jnp.where((n >= lo) & (n < hi), ext_ref[pl.ds(PACK + dk, tm), :], 0.0) * w[k:k + 1, :]
        o_ref[0] = _silu(acc).astype(o_ref.dtype)

    @pl.when(jnp.logical_not(has_edge))
    def _():
        acc = centre
        for k, dk in taps:
            acc = acc + ext_ref[pl.ds(PACK + dk, tm), :] * w[k:k + 1, :]
        o_ref[0] = _silu(acc).astype(o_ref.dtype)


def _conv_silu(u, col0, width, w, bias, *, t_lat, tm, tc=512):
    b, r, _ = u.shape
    ksize = w.shape[0]
    cb = col0 // tc
    prev, nxt = _halo_specs(tm, tc, lambda j: cb + j, r, PACK)
    return pl.pallas_call(
        functools.partial(_conv_silu_kernel, tm=tm, ksize=ksize, t_lat=t_lat, r_tot=r),
        grid=(b, r // tm, width // tc),
        in_specs=[
            prev,
            pl.BlockSpec((1, tm, tc), lambda bb, i, j: (bb, i, cb + j)),
            nxt,
            pl.BlockSpec((ksize, tc), lambda bb, i, j: (0, j)),
            pl.BlockSpec((1, tc), lambda bb, i, j: (0, j)),
        ],
        out_specs=pl.BlockSpec((1, tm, tc), lambda bb, i, j: (bb, i, j)),
        out_shape=jax.ShapeDtypeStruct((b, r, width), BF16),
        scratch_shapes=[pltpu.VMEM((tm + 2 * PACK, tc), F32)],
        compiler_params=_cp(("arbitrary", "arbitrary", "arbitrary")),
        name="conv_silu",
    )(u, u, u, w, bias.reshape(1, width))


def _level_masks(c):
    t = np.arange(c)[:, None]
    s = np.arange(c)[None, :]
    ms = [(t == s)]
    h = 1
    while h < c:
        g = 2 * h
        ms.append((t // g == s // g) & ((t % g) >= h) & ((s % g) < h))
        h *= 2
    fwd = np.stack(ms).astype(np.float32)
    return np.stack([fwd, np.transpose(fwd, (0, 2, 1))])


def _hgrn_dir(q_raw, v, f_raw, lbp, masks_ref, d, st_ref, o_ref, rev, c, nheads, hd):
    width = nheads * hd
    llb, l1m, oml = lbp[0:1, :], lbp[1:2, :], lbp[2:3, :]
    q = _silu(q_raw)
    e = jnp.exp(-jnp.abs(f_raw))
    inv = 1.0 / (1.0 + e)
    k = oml * jnp.where(f_raw >= 0, e * inv, inv)
    a = l1m + jnp.minimum(f_raw, 0.0) - jnp.log(1.0 + e)
    logf = jnp.maximum(llb, a) + jnp.log(1.0 + jnp.exp(-jnp.abs(llb - a)))

    t_idx = lax.broadcasted_iota(jnp.int32, (c, 1), 0)
    p = logf
    tot = logf
    levels = []
    h = 1
    while h < SUBLANE:
        up = (t_idx & h) != 0
        tgt = jnp.logical_not(up) if rev else up
        levels.append(jnp.exp(jnp.where(tgt, p, tot - p)))
        sib = jnp.where(up, pltpu.roll(tot, h, 0), pltpu.roll(tot, c - h, 0))
        p = p + jnp.where(tgt, sib, 0.0)
        tot = tot + sib
        h *= 2
    nblk = c // SUBLANE
    pb = [p[b * SUBLANE:(b + 1) * SUBLANE] for b in range(nblk)]
    tb = [tot[b * SUBLANE:(b + 1) * SUBLANE] for b in range(nblk)]
    m = 1
    while m < nblk:
        tgt = [((b // m) % 2 == 1) != rev for b in range(nblk)]
        levels.append(jnp.exp(jnp.concatenate([pb[b] if tgt[b] else tb[b] - pb[b] for b in range(nblk)], axis=0)))
        pb = [pb[b] + tb[b ^ m] if tgt[b] else pb[b] for b in range(nblk)]
        pair = {}
        for b in range(nblk):
            if min(b, b ^ m) not in pair:
                pair[min(b, b ^ m)] = tb[b] + tb[b ^ m]
        tb = [pair[min(b, b ^ m)] for b in range(nblk)]
        m *= 2
    p = jnp.concatenate(pb, axis=0)
    eb = jnp.exp(p)
    ek = jnp.exp(jnp.concatenate([tb[b] - pb[b] for b in range(nblk)], axis=0))
    etot = jnp.exp(tb[0][0:1, :])

    q16, k16 = q.astype(BF16), k.astype(BF16)
    levels16 = [lev.astype(BF16) for lev in levels]
    for hh in range(nheads):
        sl = slice(hh * hd, (hh + 1) * hd)
        qh, kh, vh = q[:, sl], k[:, sl], v[:, sl]
        a_mat = _dot_nt(q16[:, sl], k16[:, sl]) * masks_ref[d, 0]
        for li, lev in enumerate(levels16):
            eh = lev[:, sl]
            a_mat = a_mat + _dot_nt(q16[:, sl] * eh, k16[:, sl] * eh) * masks_ref[d, li + 1]
        st = st_ref[d, hh]
        o = _dot(a_mat.astype(BF16), vh.astype(BF16))
        o = o + _dot_nt((qh * eb[:, sl]).astype(BF16), st.astype(BF16))
        o_ref[0, :, sl] = o.astype(o_ref.dtype)
        khat = kh * ek[:, sl]
        st_ref[d, hh] = st * etot[:, sl] + _dot(vh.T.astype(BF16), khat.astype(BF16))


def _hgrn_kernel(qf_ref, if_ref, ff_ref, qb_ref, ib_ref, fb_ref, lbf_ref, lbb_ref, masks_ref,
                 of_ref, ob_ref, st_ref, *, c, nheads, hd):
    @pl.when(pl.program_id(1) == 0)
    def _():
        st_ref[...] = jnp.zeros_like(st_ref)

    ld = lambda ref: ref[0].astype(F32)
    _hgrn_dir(ld(qf_ref), ld(if_ref), ld(ff_ref), lbf_ref[...], masks_ref, 0, st_ref, of_ref, False, c, nheads, hd)
    _hgrn_dir(ld(qb_ref), ld(ib_ref), ld(fb_ref), lbb_ref[...], masks_ref, 1, st_ref, ob_ref, True, c, nheads, hd)


def _scan_chunk_maps(n_lat, n_ctx):
    n = n_lat + n_ctx
    fwd = lambda j: (j + n_lat) % n
    bwd = lambda j: n - 1 - j
    return n, fwd, bwd


def _hgrn_scan(u, cq, ci, cff, cfb, lbp_f, lbp_b, *, t_lat, c=CHUNK, nheads=A_HEADS, hd=A_DIM):
    b, r, _ = u.shape
    width = nheads * hd
    n, fwd, bwd = _scan_chunk_maps(t_lat // c, (r - t_lat) // c)
    masks = jnp.asarray(_level_masks(c))

    def spec(col, cm):
        return pl.BlockSpec((1, c, width), lambda bb, j: (bb, cm(j), col // width))

    const2 = pl.BlockSpec((3, width), lambda bb, j: (0, 0))
    out_shape = jax.ShapeDtypeStruct((b, r, width), BF16)
    return pl.pallas_call(
        functools.partial(_hgrn_kernel, c=c, nheads=nheads, hd=hd),
        grid=(b, n),
        in_specs=[spec(cq, fwd), spec(ci, fwd), spec(cff, fwd), spec(cq, bwd), spec(ci, bwd), spec(cfb, bwd),
                  const2, const2,
                  pl.BlockSpec(masks.shape, lambda bb, j: (0, 0, 0, 0))],
        out_specs=[pl.BlockSpec((1, c, width), lambda bb, j: (bb, fwd(j), 0)),
                   pl.BlockSpec((1, c, width), lambda bb, j: (bb, bwd(j), 0))],
        out_shape=[out_shape, out_shape],
        scratch_shapes=[pltpu.VMEM((2, nheads, hd, hd), F32)],
        compiler_params=_cp(("arbitrary", "arbitrary")),
        name="hgrn2_scan",
    )(u, u, u, u, u, u, lbp_f, lbp_b, masks)


def _ssd_dir(xm, bm, cm, dtc_raw, dtr_raw, prow, pcol, tri_ref, d, st_ref, o_ref, rev, c):
    nh = M_HEADS
    dt_c = _softplus(dtc_raw + prow[0:1, :])
    lf_c = dt_c * prow[1:2, :]
    dt_r = _softplus(dtr_raw + pcol[:, 0:1])[d * nh:(d + 1) * nh]
    lf_r = dt_r * pcol[d * nh:(d + 1) * nh, 1:2]
    lo, up = tri_ref[0], tri_ref[1]
    if rev:
        b_c = _dot_exact(up, lf_c)
        b_r = _dot_exact(lf_r, lo)
        causal = up > 0.5
        btot = b_c[0:1, :]
    else:
        b_c = _dot_exact(lo, lf_c)
        b_r = _dot_exact(lf_r, up)
        causal = lo > 0.5
        btot = b_c[c - 1:c, :]
    lane = lax.broadcasted_iota(jnp.int32, (1, LANE), 1)
    left = lane < M_HEADDIM
    bd_mask = jnp.concatenate([jnp.broadcast_to(left, (M_STATE, LANE)),
                               jnp.broadcast_to(jnp.logical_not(left), (M_STATE, LANE))], axis=0)
    for g in range(M_GROUPS):
        cg = cm[:, g * M_STATE:(g + 1) * M_STATE]
        bg_t = bm[:, g * M_STATE:(g + 1) * M_STATE].T
        gmat = _dot(cg.astype(BF16), bg_t.astype(BF16))
        for pp in range(nh // (2 * M_GROUPS)):
            pair = g * (nh // (2 * M_GROUPS)) + pp
            lhs, qs, kts, decs = [], [], [], []
            for h in (2 * pair, 2 * pair + 1):
                hc = d * nh + h
                bcol = b_c[:, hc:hc + 1]
                brow = b_r[h:h + 1, :]
                dtrow = dt_r[h:h + 1, :]
                dec = jnp.where(causal, jnp.exp(bcol - brow), 0.0) * dtrow
                lhs.append(gmat * dec)
                qs.append(cg * jnp.exp(bcol))
                bt = btot[:, hc:hc + 1]
                kts.append(bg_t * (jnp.exp(bt - brow) * dtrow))
                decs.append(jnp.broadcast_to(jnp.exp(bt), (M_STATE, LANE)))
            xp = xm[:, pair * LANE:(pair + 1) * LANE]
            rhs = jnp.concatenate([jnp.where(left, xp, 0.0), jnp.where(left, 0.0, xp)], axis=0)
            st = st_ref[d, pair]
            o = _dot(jnp.concatenate(lhs, axis=1).astype(BF16), rhs.astype(BF16))
            o = o + _dot(jnp.concatenate(qs, axis=1).astype(BF16), st.astype(BF16))
            o_ref[0, :, pair * LANE:(pair + 1) * LANE] = o.astype(o_ref.dtype)
            upd = _dot(jnp.concatenate(kts, axis=0).astype(BF16), xp.astype(BF16))
            st_ref[d, pair] = st * jnp.concatenate(decs, axis=0) + jnp.where(bd_mask, upd, 0.0)


def _ssd_kernel(xf_ref, bf_ref, cf_ref, dcf_ref, drf_ref, xb_ref, bb_ref, cb_ref, dcb_ref, drb_ref,
                prow_ref, pcol_ref, tri_ref, of_ref, ob_ref, st_ref, *, c):
    @pl.when(pl.program_id(1) == 0)
    def _():
        st_ref[...] = jnp.zeros_like(st_ref)

    ld = lambda ref: ref[0].astype(F32)
    _ssd_dir(ld(xf_ref), ld(bf_ref), ld(cf_ref), ld(dcf_ref), drf_ref[0], prow_ref[...], pcol_ref[...],
             tri_ref, 0, st_ref, of_ref, False, c)
    _ssd_dir(ld(xb_ref), ld(bb_ref), ld(cb_ref), ld(dcb_ref), drb_ref[0], prow_ref[...], pcol_ref[...],
             tri_ref, 1, st_ref, ob_ref, True, c)


def _tri_mats(c):
    t = np.arange(c)[:, None]
    s = np.arange(c)[None, :]
    return np.stack([(s <= t), (s >= t)]).astype(np.float32)


def _ssd_scan(xbc, u, dt_t, prow, pcol, *, t_lat, c=CHUNK):
    b, r, _ = xbc.shape
    n, fwd, bwd = _scan_chunk_maps(t_lat // c, (r - t_lat) // c)
    gs = M_GROUPS * M_STATE
    tri = jnp.asarray(_tri_mats(c))

    def specs(cm):
        return [pl.BlockSpec((1, c, M_INNER), lambda bb, j: (bb, cm(j), 0)),
                pl.BlockSpec((1, c, gs), lambda bb, j: (bb, cm(j), M_INNER // gs)),
                pl.BlockSpec((1, c, gs), lambda bb, j: (bb, cm(j), M_INNER // gs + 1)),
                pl.BlockSpec((1, c, LANE), lambda bb, j: (bb, cm(j), U_DT // LANE)),
                pl.BlockSpec((1, 2 * M_HEADS, c), lambda bb, j: (bb, 0, cm(j)))]

    out_shape = jax.ShapeDtypeStruct((b, r, M_INNER), BF16)
    return pl.pallas_call(
        functools.partial(_ssd_kernel, c=c),
        grid=(b, n),
        in_specs=specs(fwd) + specs(bwd) + [
            pl.BlockSpec(prow.shape, lambda bb, j: (0, 0)),
            pl.BlockSpec(pcol.shape, lambda bb, j: (0, 0)),
            pl.BlockSpec(tri.shape, lambda bb, j: (0, 0, 0))],
        out_specs=[pl.BlockSpec((1, c, M_INNER), lambda bb, j: (bb, fwd(j), 0)),
                   pl.BlockSpec((1, c, M_INNER), lambda bb, j: (bb, bwd(j), 0))],
        out_shape=[out_shape, out_shape],
        scratch_shapes=[pltpu.VMEM((2, M_HEADS // 2, 2 * M_STATE, LANE), F32)],
        compiler_params=_cp(("arbitrary", "arbitrary")),
        name="ssd_scan",
    )(xbc, xbc, xbc, u, dt_t, xbc, xbc, xbc, u, dt_t, prow, pcol, tri)


def _band_mats(windows, tb, period):
    t = np.arange(tb)[:, None]
    s = np.arange(tb)[None, :]
    out = []
    for w in windows:
        left = w // 2
        right = w - 1 - left
        out.append((t // period == s // period) & (s >= t - left) & (s <= t + right))
    return np.stack(out).astype(np.float32)


def _clip_count(idx, n, left, right):
    return jnp.minimum(idx + right + 1, n) - jnp.maximum(idx - left, 0)


def _pool_kernel(z_ref, band_ref, gw_ref, sc_ref, o_ref, pad_ref, *, n, tb, grid_w, windows):
    g = pl.program_id(1)
    band = band_ref[0].astype(BF16)
    shift = int(math.log2(grid_w)) if grid_w else 0
    for gi, w in enumerate(windows):
        @pl.when(g == gi)
        def _(w=w):
            left = w // 2
            right = w - 1 - left
            wr = w if grid_w else 1
            top = left * grid_w
            if grid_w:
                pad_ref[0:top, :] = jnp.zeros((top, LANE), F32)
                if right:
                    pad_ref[top + n:top + n + right * grid_w, :] = jnp.zeros((right * grid_w, LANE), F32)

            def col_sum(i, carry):
                pad_ref[pl.ds(top + i * tb, tb), :] = _dot(band, z_ref[0, pl.ds(i * tb, tb), :].astype(BF16))
                return carry

            unroll = min(4, n // tb)
            lax.fori_loop(0, n // tb, col_sum, 0, unroll=unroll)

            def finish(i, carry):
                acc = pad_ref[pl.ds(i * tb, tb), :]
                for kk in range(1, wr):
                    acc = acc + pad_ref[pl.ds(i * tb + kk * grid_w, tb), :]
                t = i * tb + lax.broadcasted_iota(jnp.int32, (tb, 1), 0)
                if grid_w:
                    cnt = (_clip_count(lax.shift_right_logical(t, shift), n // grid_w, left, right)
                           * _clip_count(t & (grid_w - 1), grid_w, left, right))
                else:
                    cnt = _clip_count(t, n, left, right)
                y = acc / cnt.astype(F32) - z_ref[0, pl.ds(i * tb, tb), :].astype(F32)
                out = _dot(y.astype(BF16), gw_ref[0].astype(BF16)) * sc_ref[0]
                o_ref[0, pl.ds(i * tb, tb), :] = out.astype(o_ref.dtype)
                return carry

            lax.fori_loop(0, n // tb, finish, 0, unroll=unroll)


def _pool(u, group_w, scale, *, row0, n, grid_w):
    b = u.shape[0]
    tb = 2 * grid_w if grid_w else n
    bands = jnp.asarray(_band_mats(P_WINDOWS, tb, grid_w if grid_w else n))
    rb = row0 // n
    pad_rows = n + (max(P_WINDOWS) - 1) * grid_w
    return pl.pallas_call(
        functools.partial(_pool_kernel, n=n, tb=tb, grid_w=grid_w, windows=P_WINDOWS),
        grid=(b, len(P_WINDOWS)),
        in_specs=[
            pl.BlockSpec((1, n, P_GROUP), lambda bb, g: (bb, rb, U_P // P_GROUP + g)),
            pl.BlockSpec((1, tb, tb), lambda bb, g: (g, 0, 0)),
            pl.BlockSpec((1, P_GROUP, P_GROUP), lambda bb, g: (g, 0, 0)),
            pl.BlockSpec((1, 1, P_GROUP), lambda bb, g: (g, 0, 0)),
        ],
        out_specs=pl.BlockSpec((1, n, P_GROUP), lambda bb, g: (bb, 0, g)),
        out_shape=jax.ShapeDtypeStruct((b, n, P_WIDTH), BF16),
        scratch_shapes=[pltpu.VMEM((pad_rows, LANE), F32)],
        compiler_params=_cp(("arbitrary", "arbitrary")),
        name="pool_grid" if grid_w else "pool_seq",
    )(u, bands, group_w, scale.reshape(len(P_WINDOWS), 1, P_GROUP))


def _chan_dft_mats():
    k = np.arange(F_WIDTH // 4)
    ang = 2.0 * np.pi * ((k[:, None] * k[None, :]) % len(k)) / len(k)
    eye = np.eye(4)
    w = np.concatenate([np.kron(eye, np.cos(ang)), np.kron(eye, np.sin(ang))], axis=1)
    hi = w.astype(np.float32).astype(BF16)
    lo = (w - hi.astype(np.float64)).astype(np.float32).astype(BF16)
    return jnp.asarray(hi), jnp.asarray(lo)


def _chan_dft_kernel(z_ref, wh_ref, wl_ref, o_ref):
    z = z_ref[0]
    o_ref[0] = (_dot(z, wh_ref[...]) + _dot(z, wl_ref[...])).astype(o_ref.dtype)


def _chan_dft(u, *, tm):
    b, r, _ = u.shape
    wh, wl = _chan_dft_mats()
    wspec = pl.BlockSpec(wh.shape, lambda bb, i: (0, 0))
    return pl.pallas_call(
        _chan_dft_kernel,
        grid=(b, r // tm),
        in_specs=[pl.BlockSpec((1, tm, F_WIDTH), lambda bb, i: (bb, i, U_F // F_WIDTH)), wspec, wspec],
        out_specs=pl.BlockSpec((1, tm, 2 * F_WIDTH), lambda bb, i: (bb, i, 0)),
        out_shape=jax.ShapeDtypeStruct((b, r, 2 * F_WIDTH), BF16),
        compiler_params=_cp(("arbitrary", "arbitrary")),
        name="chan_dft",
    )(u, wh, wl)


def _dft_cols(n, ncols):
    k = jnp.arange(n, dtype=jnp.int32)[:, None]
    j = jnp.arange(ncols, dtype=jnp.int32)[None, :]
    ang = ((k * j) & (n - 1)).astype(F32) * (2.0 * math.pi / n)
    return jnp.cos(ang), jnp.sin(ang)


ROT_PERIOD = 16


def _rot_patterns(n, tk):
    assert ROT_PERIOD % (n // tk) == 0
    k = np.arange(ROT_PERIOD)[None, :, None]
    t0 = (np.arange(n // tk) * tk)[:, None, None]
    ang = 2.0 * np.pi * ((k * t0) % n) / n
    shape = (n // tk, ROT_PERIOD, tk)
    return (jnp.asarray(np.broadcast_to(np.cos(ang), shape), F32), jnp.asarray(np.broadcast_to(np.sin(ang), shape), F32))


def _fold_mats(tf):
    flip = np.zeros((tf, tf), np.float32)
    flip[np.arange(1, tf), tf - np.arange(1, tf)] = 1.0
    first = np.zeros((tf, PACK), np.float32)
    first[0, 0] = 1.0
    return jnp.asarray(flip, BF16), jnp.asarray(first, BF16)


def _dft_fold_kernel(cur_ref, mir_ref, nxt_ref, flip_ref, first_ref, o_ref):
    tf = cur_ref.shape[1]
    mirrored = _dot(flip_ref[...], mir_ref[0]) + _dot(first_ref[...], nxt_ref[0])
    row = pl.program_id(1) * tf + lax.broadcasted_iota(jnp.int32, (tf, 1), 0)
    mirrored = jnp.where(row == 0, 0.0, mirrored)
    cur = cur_ref[0].astype(F32)
    sign = jnp.where(lax.broadcasted_iota(jnp.int32, (1, 2 * F_WIDTH), 1) < F_WIDTH, 1.0, -1.0)
    o_ref[0] = (cur + sign * mirrored).astype(o_ref.dtype)


def _dft_fold(zcs, *, n, tf=512):
    b, r, w = zcs.shape
    nt = n // tf
    flip, first = _fold_mats(tf)
    last16 = r // PACK - 1
    return pl.pallas_call(
        _dft_fold_kernel,
        grid=(b, nt // 2),
        in_specs=[pl.BlockSpec((1, tf, w), lambda bb, i: (bb, i, 0)),
                  pl.BlockSpec((1, tf, w), lambda bb, i: (bb, nt - 1 - i, 0)),
                  pl.BlockSpec((1, PACK, w), lambda bb, i: (bb, jnp.minimum((nt - i) * (tf // PACK), last16), 0)),
                  pl.BlockSpec(flip.shape, lambda bb, i: (0, 0)),
                  pl.BlockSpec(first.shape, lambda bb, i: (0, 0))],
        out_specs=pl.BlockSpec((1, tf, w), lambda bb, i: (bb, i, 0)),
        out_shape=jax.ShapeDtypeStruct((b, n // 2, w), BF16),
        compiler_params=_cp(("arbitrary", "arbitrary")),
        name="dft_fold",
    )(zcs, zcs, zcs, flip, first)


def _time_dft_kernel(cb_ref, sb_ref, pc_ref, ps_ref, zc_ref, zs_ref, zh_ref, o_ref, acc_ref,
                     *, nb, tm, tk, scale, folded):
    kk = pl.program_id(1)

    @pl.when(kk == 0)
    def _():
        acc_ref[...] = jnp.zeros_like(acc_ref)

    cb = cb_ref[...].reshape(tm // ROT_PERIOD, ROT_PERIOD, tk)
    sb = sb_ref[...].reshape(tm // ROT_PERIOD, ROT_PERIOD, tk)
    ca, sa = pc_ref[...], ps_ref[...]
    ct = (ca * cb - sa * sb).reshape(tm, tk).astype(BF16)
    st = (sa * cb + ca * sb).reshape(tm, tk).astype(BF16)
    for bb in range(nb):
        acc_ref[bb] += _dot(ct, zc_ref[bb]) - _dot(st, zs_ref[bb])

    @pl.when(kk == pl.num_programs(1) - 1)
    def _():
        y = acc_ref[...]
        if folded:
            row = lax.broadcasted_iota(jnp.int32, (1, tm, 1), 1)
            y = y + jnp.where((row & 1) == 0, 1.0, -1.0) * zh_ref[:, 0:1, :].astype(F32)
        o_ref[...] = (y * scale).astype(o_ref.dtype)


def _time_dft(zcs, *, row0, n, tm, tk, folded):
    b = zcs.shape[0]
    scale = 1.0 / math.sqrt(n * (F_WIDTH // 4))
    cb, sb = _dft_cols(n, tk)
    pc, ps = _rot_patterns(n, tk)
    pat = pl.BlockSpec((None, ROT_PERIOD, tk), lambda i, k: (k, 0, 0))
    z = _dft_fold(zcs, n=n) if folded else zcs
    zrow = 0 if folded else row0 // tk
    return pl.pallas_call(
        functools.partial(_time_dft_kernel, nb=b, tm=tm, tk=tk, scale=scale, folded=folded),
        grid=(n // tm, (n // 2 if folded else n) // tk),
        in_specs=[
            pl.BlockSpec((tm, tk), lambda i, k: (i, 0)),
            pl.BlockSpec((tm, tk), lambda i, k: (i, 0)),
            pat, pat,
            pl.BlockSpec((b, tk, F_WIDTH), lambda i, k: (0, zrow + k, 0)),
            pl.BlockSpec((b, tk, F_WIDTH), lambda i, k: (0, zrow + k, 1)),
            pl.BlockSpec((b, PACK, F_WIDTH), lambda i, k: (0, (row0 + n // 2) // PACK, 0)),
        ],
        out_specs=pl.BlockSpec((b, tm, F_WIDTH), lambda i, k: (0, i, 0)),
        out_shape=jax.ShapeDtypeStruct((b, n, F_WIDTH), BF16),
        scratch_shapes=[pltpu.VMEM((b, tm, F_WIDTH), F32)],
        compiler_params=_cp(("arbitrary", "arbitrary")),
        name="time_dft_%d" % n,
    )(cb, sb, pc, ps, z, z, zcs)


def _merge_kernel(oaf_ref, oab_ref, g_ref, omf_ref, omb_ref, xm_ref, z0_ref, z1_ref, pm_ref, fm_ref, pmc_ref, fmc_ref,
                  anw_ref, dsk_ref, mnw_ref, wbr_ref, g0_ref, g1_ref, g2_ref, g3_ref, o_ref, act_ref):
    @pl.when(pl.program_id(2) == 0)
    def _():
        ld = lambda ref: ref[0].astype(F32)
        oa = ld(oaf_ref) + ld(oab_ref)
        gate = _silu(ld(g_ref))
        for hh in range(A_HEADS):
            sl = slice(hh * A_DIM, (hh + 1) * A_DIM)
            oh = oa[:, sl]
            ms = jnp.mean(oh * oh, axis=-1, keepdims=True)
            act_ref[:, sl] = (oh * lax.rsqrt(ms + EPS) * anw_ref[...] * gate[:, sl]).astype(BF16)
        y = ld(omf_ref) + ld(omb_ref) + dsk_ref[...] * ld(xm_ref)
        z = jnp.concatenate([ld(z0_ref), ld(z1_ref)], axis=1)
        y = y * _silu(z)
        ms = jnp.mean(y * y, axis=-1, keepdims=True)
        act_ref[:, A_WIDTH:A_WIDTH + M_INNER] = (y * lax.rsqrt(ms + EPS) * mnw_ref[...]).astype(BF16)
        act_ref[:, A_WIDTH + M_INNER:A_WIDTH + M_INNER + P_WIDTH] = pm_ref[0].astype(BF16)
        act_ref[:, A_WIDTH + M_INNER + P_WIDTH:] = fm_ref[0].astype(BF16)

        @pl.when(pl.program_id(1) == pl.num_programs(1) - 1)
        def _():
            t_ctx = pmc_ref.shape[1]
            tail = slice(act_ref.shape[0] - t_ctx, act_ref.shape[0])
            act_ref[tail, A_WIDTH + M_INNER:A_WIDTH + M_INNER + P_WIDTH] = pmc_ref[0].astype(BF16)
            act_ref[tail, A_WIDTH + M_INNER + P_WIDTH:] = fmc_ref[0].astype(BF16)

    offs = (0, A_WIDTH, A_WIDTH + M_INNER, A_WIDTH + M_INNER + P_WIDTH, ACT_W)
    acc = None
    for br, gr in enumerate((g0_ref, g1_ref, g2_ref, g3_ref)):
        yb = _dot(act_ref[:, offs[br]:offs[br + 1]], wbr_ref[offs[br]:offs[br + 1], :])
        term = (jnp.tanh(gr[0].astype(F32)) + 1.0) * yb
        acc = term if acc is None else acc + term
    o_ref[0] = acc.astype(o_ref.dtype)


def _merge(u, oaf, oab, omf, omb, xbc, pm, fm, pmc, fmc, anw, dsk, mnw, wbr, *, tm, tn=512):
    b, r, _ = u.shape
    t_ctx = pmc.shape[1]
    assert r - pm.shape[1] == t_ctx and t_ctx <= tm and r % tm == 0
    ctx_rows = pl.BlockSpec((1, t_ctx, P_WIDTH), lambda bb, i, j: (bb, 0, 0))

    def rows(width, cb):
        return pl.BlockSpec((1, tm, width), lambda bb, i, j: (bb, i, cb))

    def vec(width):
        return pl.BlockSpec((1, width), lambda bb, i, j: (0, 0))

    def gate(br):
        return pl.BlockSpec((1, tm, tn), lambda bb, i, j: (bb, i, (U_GATE + br * D_MODEL) // tn + j))

    return pl.pallas_call(
        _merge_kernel,
        grid=(b, r // tm, D_MODEL // tn),
        in_specs=[rows(A_WIDTH, 0), rows(A_WIDTH, 0), rows(A_WIDTH, U_G // A_WIDTH),
                  rows(M_INNER, 0), rows(M_INNER, 0), rows(M_INNER, 0),
                  rows(512, U_Z // 512), rows(512, U_Z // 512 + 1),
                  rows(P_WIDTH, 0), rows(F_WIDTH, 0), ctx_rows, ctx_rows,
                  vec(A_DIM), vec(M_INNER), vec(M_INNER),
                  pl.BlockSpec((ACT_W, tn), lambda bb, i, j: (0, j)),
                  gate(0), gate(1), gate(2), gate(3)],
        out_specs=pl.BlockSpec((1, tm, tn), lambda bb, i, j: (bb, i, j)),
        out_shape=jax.ShapeDtypeStruct((b, r, D_MODEL), BF16),
        scratch_shapes=[pltpu.VMEM((tm, ACT_W), BF16)],
        compiler_params=_cp(("arbitrary", "arbitrary", "arbitrary")),
        name="merge",
    )(oaf, oab, u, omf, omb, xbc, u, u, pm, fm, pmc, fmc, anw, dsk, mnw, wbr, u, u, u, u)


def _proj_resid_kernel(y_ref, w_ref, x_ref, gl_ref, gc_ref, o_ref, *, tm, t_lat):
    is_ctx = _row_is_ctx(pl.program_id(1) * tm, tm, t_lat)
    gate = jnp.where(is_ctx, gc_ref[...], gl_ref[0])
    o_ref[0] = x_ref[0] + gate * _dot(y_ref[0], w_ref[...])


def _proj_resid(y, w, x, gl, gc, *, t_lat, tm):
    b, r, d = x.shape
    k = y.shape[2]
    return pl.pallas_call(
        functools.partial(_proj_resid_kernel, tm=tm, t_lat=t_lat),
        grid=(b, r // tm),
        in_specs=[pl.BlockSpec((1, tm, k), lambda bb, i: (bb, i, 0)),
                  pl.BlockSpec((k, d), lambda bb, i: (0, 0)),
                  pl.BlockSpec((1, tm, d), lambda bb, i: (bb, i, 0)),
                  pl.BlockSpec((1, 1, d), lambda bb, i: (bb, 0, 0)),
                  pl.BlockSpec((1, d), lambda bb, i: (0, 0))],
        out_specs=pl.BlockSpec((1, tm, d), lambda bb, i: (bb, i, 0)),
        out_shape=jax.ShapeDtypeStruct((b, r, d), F32),
        compiler_params=_cp(("arbitrary", "arbitrary")),
        name="proj_resid",
    )(y, w, x, gl, gc)


def _conv3(ext, s_ref, w, bias, tm, m_prev, m_next):
    main = ext[0:tm]
    s_ref[SUBLANE:SUBLANE + tm, :] = main
    s_ref[SUBLANE - 1:SUBLANE, :] = ext[tm + SUBLANE - 1:tm + SUBLANE]
    s_ref[SUBLANE + tm:SUBLANE + tm + 1, :] = ext[tm + SUBLANE:tm + SUBLANE + 1]
    prev = s_ref[pl.ds(SUBLANE - 1, tm), :]
    nxt = s_ref[pl.ds(SUBLANE + 1, tm), :]
    return main * w[1:2, :] + (prev * m_prev) * w[0:1, :] + (nxt * m_next) * w[2:3, :] + bias


def _ffn_kernel(xp_ref, x_ref, xn_ref, nw_ref, shl_ref, scl_ref, shc_ref, scc_ref, gl_ref, gc_ref, fnw_ref,
                wa_ref, wb_ref, cwa_ref, cwb_ref, ba_ref, bb_ref, wd_ref, o_ref, h_ref, sa_ref, sb_ref,
                *, tm, t_lat, r_tot, final_norm):
    i = pl.program_id(1)
    kk = pl.program_id(2)
    row0 = i * tm

    @pl.when(kk == 0)
    def _():
        vecs = (nw_ref[...], shl_ref[0], scl_ref[0], shc_ref[...], scc_ref[...])
        _norm_mod_tile(x_ref, h_ref, tm, row0, t_lat, *vecs)
        halo = jnp.concatenate([xp_ref[0], xn_ref[0]], axis=0)
        hrow = lax.broadcasted_iota(jnp.int32, (2 * SUBLANE, 1), 0)
        grow = jnp.where(hrow < SUBLANE, row0 - SUBLANE + hrow, row0 + tm - SUBLANE + hrow)
        h_ref[tm:tm + 2 * SUBLANE, :] = _norm_mod(halo, grow >= t_lat, *vecs).astype(BF16)
        o_ref[...] = jnp.zeros_like(o_ref)

    t = row0 + lax.broadcasted_iota(jnp.int32, (tm, 1), 0)
    m_prev = jnp.where((t == 0) | (t == t_lat), 0.0, 1.0)
    m_next = jnp.where((t == t_lat - 1) | (t == r_tot - 1), 0.0, 1.0)
    h = h_ref[...]
    a = _conv3(_dot(h, wa_ref[...]), sa_ref, cwa_ref[...], ba_ref[...], tm, m_prev, m_next)
    g = _conv3(_dot(h, wb_ref[...]), sb_ref, cwb_ref[...], bb_ref[...], tm, m_prev, m_next)
    o_ref[0] += _dot((_silu(a) * g).astype(BF16), wd_ref[...])

    @pl.when(kk == pl.num_programs(2) - 1)
    def _():
        gate = jnp.where(_row_is_ctx(row0, tm, t_lat), gc_ref[...], gl_ref[0])
        y = x_ref[0] + gate * o_ref[0]
        if final_norm:
            ms = jnp.mean(y * y, axis=-1, keepdims=True)
            y = y * lax.rsqrt(ms + EPS) * fnw_ref[...]
        o_ref[0] = y


def _ffn(x, nw, shl, scl, shc, scc, gl, gc, fnw, w_up, cw, cb, wd, *, t_lat, tm, tk, final_norm, out_rows):
    b, r, d = x.shape
    nk = D_FF // tk
    x_prev, x_next = _halo_specs(tm, d, lambda k: 0, r, SUBLANE)
    vec_l = pl.BlockSpec((1, 1, d), lambda bb, i, k: (bb, 0, 0))
    vec_c = pl.BlockSpec((1, d), lambda bb, i, k: (0, 0))
    cb2 = cb.reshape(1, 2 * D_FF)
    return pl.pallas_call(
        functools.partial(_ffn_kernel, tm=tm, t_lat=t_lat, r_tot=r, final_norm=final_norm),
        grid=(b, r // tm, nk),
        in_specs=[x_prev, pl.BlockSpec((1, tm, d), lambda bb, i, k: (bb, i, 0)), x_next,
                  vec_c, vec_l, vec_l, vec_c, vec_c, vec_l, vec_c, vec_c,
                  pl.BlockSpec((d, tk), lambda bb, i, k: (0, k)),
                  pl.BlockSpec((d, tk), lambda bb, i, k: (0, nk + k)),
                  pl.BlockSpec((FFN_CONV, tk), lambda bb, i, k: (0, k)),
                  pl.BlockSpec((FFN_CONV, tk), lambda bb, i, k: (0, nk + k)),
                  pl.BlockSpec((1, tk), lambda bb, i, k: (0, k)),
                  pl.BlockSpec((1, tk), lambda bb, i, k: (0, nk + k)),
                  pl.BlockSpec((tk, d), lambda bb, i, k: (k, 0))],
        out_specs=pl.BlockSpec((1, tm, d), lambda bb, i, k: (bb, i, 0)),
        out_shape=jax.ShapeDtypeStruct((b, out_rows, d), F32),
        scratch_shapes=[pltpu.VMEM((tm + 2 * SUBLANE, d), BF16),
                        pltpu.VMEM((tm + 2 * SUBLANE, tk), F32),
                        pltpu.VMEM((tm + 2 * SUBLANE, tk), F32)],
        compiler_params=_cp(("arbitrary", "arbitrary", "arbitrary")),
        name="ffn",
    )(x, x, x, nw, shl, scl, shc, scc, gl, gc, fnw, w_up, w_up, cw, cw, cb2, cb2, wd)


def _arrange_w_in(w):
    pad = jnp.zeros((w.shape[0], NU - W_END), BF16)
    cols = lambda lo, hi: w[:, lo:hi].astype(BF16)
    return jnp.concatenate([(0.5 * w[:, W_GATE:W_END]).astype(BF16), cols(W_A, W_DT), cols(W_P, W_GATE),
                            cols(W_DT, W_P), pad], axis=1)


def kernel(x, c, ctx, c_ctx, w_ada, b_ada, norm1_w, w_in, a_lb_logits, a_norm_w, w_br_a, m_conv_w, m_conv_b,
           m_dt_bias, m_a_log, m_d, m_norm_w, w_br_m, p_group_w, p_scale, w_br_p, w_br_f, w_out, norm2_w,
           w_up, ffn_conv_w, ffn_conv_b, w_down, final_norm_w):
    nb, t_lat, d = x.shape
    t_ctx = ctx.shape[1]
    depth = w_ada.shape[0]
    tm_in, tm = 1056, 768
    assert t_lat % CHUNK == 0 and t_ctx % CHUNK == 0 and (t_lat + t_ctx) % tm_in == 0 and (t_lat + t_ctx) % tm == 0

    lb_all = jnp.cumsum(jax.nn.softmax(a_lb_logits.astype(F32), axis=0), axis=0)
    lb_all = lb_all - lb_all[0]

    xs = jnp.concatenate([x, ctx], axis=1)
    c_all = jnp.concatenate([c, c_ctx[None, :], jnp.zeros((SUBLANE - nb - 1, d), F32)], axis=0)
    mod = _ada(c_all, w_ada, b_ada)

    for l in range(depth):
        last = l == depth - 1
        ml = mod[l, :nb].reshape(nb, 1, 6, d)
        mc = mod[l, nb].reshape(6, d)
        lat = lambda k: ml[:, :, k, :]
        cx = lambda k: mc[k:k + 1, :]

        u = _norm_mod_matmul(xs, norm1_w[l][None, :], lat(0), lat(1), cx(0), cx(1), _arrange_w_in(w_in[l]),
                             t_lat=t_lat, tm=tm_in, tn=2432)
        lbp = [jnp.stack([jnp.log(lb_all[l, dd]), jnp.log1p(-lb_all[l, dd]), 1.0 - lb_all[l, dd]])
               for dd in range(2)]
        oaf, oab = _hgrn_scan(u, U_Q, U_I, U_FF, U_FB, lbp[0], lbp[1], t_lat=t_lat)

        xbc = _conv_silu(u, U_XBC, M_XBC, m_conv_w[l], m_conv_b[l], t_lat=t_lat, tm=tm)
        dt_t = jnp.transpose(u[:, :, U_DT:U_DT + 2 * M_HEADS].astype(F32), (0, 2, 1))
        a_neg = -jnp.exp(m_a_log[l].astype(F32)).reshape(-1)
        dtb = m_dt_bias[l].astype(F32).reshape(-1)
        prow = jnp.zeros((2, LANE), F32).at[0, :2 * M_HEADS].set(dtb).at[1, :2 * M_HEADS].set(a_neg)
        pcol = jnp.stack([dtb, a_neg], axis=1)
        omf, omb = _ssd_scan(xbc, u, dt_t, prow, pcol, t_lat=t_lat)

        pm = _pool(u, p_group_w[l], p_scale[l], row0=0, n=t_lat, grid_w=GRID_W)
        pmc = _pool(u, p_group_w[l], p_scale[l], row0=t_lat, n=t_ctx, grid_w=0)
        zcs = _chan_dft(u, tm=tm)
        fm = _time_dft(zcs, row0=0, n=t_lat, tm=1024, tk=512, folded=True)
        fmc = _time_dft(zcs, row0=t_lat, n=t_ctx, tm=t_ctx, tk=t_ctx, folded=False)

        wbr = (0.5 * jnp.concatenate([w_br_a[l], w_br_m[l], w_br_p[l], w_br_f[l]], axis=0)).astype(BF16)
        dsk = jnp.repeat(m_d[l].astype(F32), M_HEADDIM)[None, :]
        anw = a_norm_w[l].astype(F32)[None, :]
        y = _merge(u, oaf, oab, omf, omb, xbc, pm, fm, pmc, fmc, anw, dsk, m_norm_w[l][None, :], wbr, tm=tm)
        xs = _proj_resid(y, w_out[l].astype(BF16), xs, lat(2), cx(2), t_lat=t_lat, tm=tm)

        xs = _ffn(xs, norm2_w[l][None, :], lat(3), lat(4), cx(3), cx(4), lat(5), cx(5), final_norm_w[None, :],
                  w_up[l].astype(BF16), ffn_conv_w[l], ffn_conv_b[l], w_down[l].astype(BF16),
                  t_lat=t_lat, tm=tm, tk=512, final_norm=last, out_rows=t_lat if last else t_lat + t_ctx)
    return xs
```

```python
import functools
import math

import numpy as np
import jax
import jax.numpy as jnp
from jax import lax
from jax.experimental import pallas as pl
from jax.experimental.pallas import tpu as pltpu

F32 = jnp.float32
BF16 = jnp.bfloat16

D_MODEL = 2048
GRID_W = 64
EPS = 1e-6

A_HEADS = 4
A_DIM = 128
A_WIDTH = A_HEADS * A_DIM
M_HEADS = 16
M_HEADDIM = 64
M_INNER = M_HEADS * M_HEADDIM
M_GROUPS = 2
M_STATE = 128
M_XBC = M_INNER + 2 * M_GROUPS * M_STATE
P_WINDOWS = (2, 4, 8, 16)
P_GROUP = 128
P_WIDTH = 512
F_WIDTH = 512
N_BRANCH = 4
D_FF = 5632
FFN_CONV = 3

W_A, W_Z, W_XBC, W_DT, W_P, W_F, W_GATE = 0, 2560, 3584, 5120, 5152, 5664, 6176
W_END = W_GATE + N_BRANCH * D_MODEL

U_GATE = 0
U_Q, U_I, U_FF, U_FB, U_G = 8192, 8704, 9216, 9728, 10240
U_Z = 10752
U_XBC = 11776
U_P = 13312
U_F = 13824
U_DT = 14336
NU = 14592
ACT_W = A_WIDTH + M_INNER + P_WIDTH + F_WIDTH

LANE = 128
SUBLANE = 8
PACK = 16
CHUNK = 128
VMEM_LIMIT = 56 * 1024 * 1024


def _cp(sem):
    return pltpu.CompilerParams(dimension_semantics=sem, vmem_limit_bytes=VMEM_LIMIT)


def _sigmoid(x):
    return 0.5 * jnp.tanh(0.5 * x) + 0.5


def _silu(x):
    return x * _sigmoid(x)


def _softplus(x):
    return jnp.maximum(x, 0.0) + jnp.log(1.0 + jnp.exp(-jnp.abs(x)))


def _dot(a, b):
    return jnp.dot(a, b, preferred_element_type=F32)


def _dot_nt(a, b):
    return lax.dot_general(a, b, (((1,), (1,)), ((), ())), preferred_element_type=F32)


def _dot_exact(a, b):
    return jnp.dot(a, b, preferred_element_type=F32, precision=lax.Precision.HIGHEST)


def _row_is_ctx(row0, tm, t_lat):
    row = row0 + lax.broadcasted_iota(jnp.int32, (tm, 1), 0)
    return row >= t_lat


def _ada_kernel(c_ref, w_ref, b_ref, o_ref):
    s = _silu(c_ref[...])
    o_ref[0] = _dot(s.astype(BF16), w_ref[0].astype(BF16)) + b_ref[0]


def _ada(c_all, w_ada, b_ada, tn=1024):
    nl, d, n6 = w_ada.shape
    return pl.pallas_call(
        _ada_kernel,
        grid=(nl, n6 // tn),
        in_specs=[
            pl.BlockSpec((SUBLANE, d), lambda l, j: (0, 0)),
            pl.BlockSpec((1, d, tn), lambda l, j: (l, 0, j)),
            pl.BlockSpec((1, 1, tn), lambda l, j: (l, 0, j)),
        ],
        out_specs=pl.BlockSpec((1, SUBLANE, tn), lambda l, j: (l, 0, j)),
        out_shape=jax.ShapeDtypeStruct((nl, SUBLANE, n6), F32),
        compiler_params=_cp(("arbitrary", "arbitrary")),
        name="ada",
    )(c_all, w_ada, b_ada.reshape(nl, 1, n6))


def _norm_mod(x, is_ctx, nw, shl, scl, shc, scc):
    ms = jnp.mean(x * x, axis=-1, keepdims=True)
    y = x * lax.rsqrt(ms + EPS) * nw
    return y * (1.0 + jnp.where(is_ctx, scc, scl)) + jnp.where(is_ctx, shc, shl)


def _norm_mod_tile(x_ref, h_ref, tm, row0, t_lat, nw, shl, scl, shc, scc):
    gain_l, gain_c = nw * (1.0 + scl), nw * (1.0 + scc)

    def body(r, carry):
        rows = pl.ds(pl.multiple_of(r * PACK, PACK), PACK)
        is_ctx = row0 + r * PACK >= t_lat
        x = x_ref[0, rows, :]
        ms = jnp.mean(x * x, axis=-1, keepdims=True)
        y = x * lax.rsqrt(ms + EPS) * jnp.where(is_ctx, gain_c, gain_l) + jnp.where(is_ctx, shc, shl)
        h_ref[rows, :] = y.astype(BF16)
        return carry

    lax.fori_loop(0, tm // PACK, body, 0, unroll=3)


def _nmm_kernel(x_ref, nw_ref, shl_ref, scl_ref, shc_ref, scc_ref, w_ref, o_ref, h_ref, *, tm, t_lat):
    @pl.when(pl.program_id(2) == 0)
    def _():
        _norm_mod_tile(x_ref, h_ref, tm, pl.program_id(1) * tm, t_lat,
                       nw_ref[...], shl_ref[0], scl_ref[0], shc_ref[...], scc_ref[...])

    o_ref[0] = _dot(h_ref[...], w_ref[...]).astype(o_ref.dtype)


def _norm_mod_matmul(x, nw, shl, scl, shc, scc, w, *, t_lat, tm, tn):
    b, r, d = x.shape
    n = w.shape[1]
    vec_l = pl.BlockSpec((1, 1, d), lambda bb, i, j: (bb, 0, 0))
    vec_c = pl.BlockSpec((1, d), lambda bb, i, j: (0, 0))
    return pl.pallas_call(
        functools.partial(_nmm_kernel, tm=tm, t_lat=t_lat),
        grid=(b, r // tm, n // tn),
        in_specs=[
            pl.BlockSpec((1, tm, d), lambda bb, i, j: (bb, i, 0)),
            vec_c, vec_l, vec_l, vec_c, vec_c,
            pl.BlockSpec((d, tn), lambda bb, i, j: (0, j)),
        ],
        out_specs=pl.BlockSpec((1, tm, tn), lambda bb, i, j: (bb, i, j)),
        out_shape=jax.ShapeDtypeStruct((b, r, n), BF16),
        scratch_shapes=[pltpu.VMEM((tm, d), BF16)],
        compiler_params=_cp(("arbitrary", "arbitrary", "arbitrary")),
        name="norm_mod_matmul",
    )(x, nw, shl, scl, shc, scc, w)


def _halo_specs(tm, tc, col_of, r_tot, halo):
    nb = tm // halo
    last = r_tot // halo - 1
    prev = pl.BlockSpec((1, halo, tc), lambda b, i, j: (b, jnp.maximum(i * nb - 1, 0), col_of(j)))
    nxt = pl.BlockSpec((1, halo, tc), lambda b, i, j: (b, jnp.minimum((i + 1) * nb, last), col_of(j)))
    return prev, nxt


def _conv_silu_kernel(p_ref, m_ref, n_ref, w_ref, b_ref, o_ref, ext_ref, *, tm, ksize, t_lat, r_tot):
    main = m_ref[0].astype(F32)
    ext_ref[0:PACK, :] = p_ref[0].astype(F32)
    ext_ref[PACK:PACK + tm, :] = main
    ext_ref[PACK + tm:2 * PACK + tm, :] = n_ref[0].astype(F32)
    i = pl.program_id(1)
    w = w_ref[...]
    centre = main * w[ksize // 2:ksize // 2 + 1, :] + b_ref[...]
    taps = [(k, k - ksize // 2) for k in range(ksize) if k != ksize // 2]
    has_edge = (i == 0) | ((i + 1) * tm >= t_lat)

    @pl.when(has_edge)
    def _():
        t = i * tm + lax.broadcasted_iota(jnp.int32, (tm, 1), 0)
        is_ctx = t >= t_lat
        lo = jnp.where(is_ctx, t_lat, 0)
        hi = jnp.where(is_ctx, r_tot, t_lat)
        acc = centre
        for k, dk in taps:
            n = t + dk
            acc = acc + jnp.where((n >= lo) & (n < hi), ext_ref[pl.ds(PACK + dk, tm), :], 0.0) * w[k:k + 1, :]
        o_ref[0] = _silu(acc).astype(o_ref.dtype)

    @pl.when(jnp.logical_not(has_edge))
    def _():
        acc = centre
        for k, dk in taps:
            acc = acc + ext_ref[pl.ds(PACK + dk, tm), :] * w[k:k + 1, :]
        o_ref[0] = _silu(acc).astype(o_ref.dtype)


def _conv_silu(u, col0, width, w, bias, *, t_lat, tm, tc=512):
    b, r, _ = u.shape
    ksize = w.shape[0]
    cb = col0 // tc
    prev, nxt = _halo_specs(tm, tc, lambda j: cb + j, r, PACK)
    return pl.pallas_call(
        functools.partial(_conv_silu_kernel, tm=tm, ksize=ksize, t_lat=t_lat, r_tot=r),
        grid=(b, r // tm, width // tc),
        in_specs=[
            prev,
            pl.BlockSpec((1, tm, tc), lambda bb, i, j: (bb, i, cb + j)),
            nxt,
            pl.BlockSpec((ksize, tc), lambda bb, i, j: (0, j)),
            pl.BlockSpec((1, tc), lambda bb, i, j: (0, j)),
        ],
        out_specs=pl.BlockSpec((1, tm, tc), lambda bb, i, j: (bb, i, j)),
        out_shape=jax.ShapeDtypeStruct((b, r, width), BF16),
        scratch_shapes=[pltpu.VMEM((tm + 2 * PACK, tc), F32)],
        compiler_params=_cp(("arbitrary", "arbitrary", "arbitrary")),
        name="conv_silu",
    )(u, u, u, w, bias.reshape(1, width))


def _level_masks(c):
    t = np.arange(c)[:, None]
    s = np.arange(c)[None, :]
    ms = [(t == s)]
    h = 1
    while h < c:
        g = 2 * h
        ms.append((t // g == s // g) & ((t % g) >= h) & ((s % g) < h))
        h *= 2
    fwd = np.stack(ms).astype(np.float32)
    return np.stack([fwd, np.transpose(fwd, (0, 2, 1))])


def _hgrn_dir(q_raw, v, f_raw, lbp, masks_ref, d, st_ref, o_ref, rev, c, nheads, hd):
    width = nheads * hd
    llb, l1m, oml = lbp[0:1, :], lbp[1:2, :], lbp[2:3, :]
    q = _silu(q_raw)
    e = jnp.exp(-jnp.abs(f_raw))
    inv = 1.0 / (1.0 + e)
    k = oml * jnp.where(f_raw >= 0, e * inv, inv)
    a = l1m + jnp.minimum(f_raw, 0.0) - jnp.log(1.0 + e)
    logf = jnp.maximum(llb, a) + jnp.log(1.0 + jnp.exp(-jnp.abs(llb - a)))

    t_idx = lax.broadcasted_iota(jnp.int32, (c, 1), 0)
    p = logf
    tot = logf
    levels = []
    h = 1
    while h < SUBLANE:
        up = (t_idx & h) != 0
        tgt = jnp.logical_not(up) if rev else up
        levels.append(jnp.exp(jnp.where(tgt, p, tot - p)))
        sib = jnp.where(up, pltpu.roll(tot, h, 0), pltpu.roll(tot, c - h, 0))
        p = p + jnp.where(tgt, sib, 0.0)
        tot = tot + sib
        h *= 2
    nblk = c // SUBLANE
    pb = [p[b * SUBLANE:(b + 1) * SUBLANE] for b in range(nblk)]
    tb = [tot[b * SUBLANE:(b + 1) * SUBLANE] for b in range(nblk)]
    m = 1
    while m < nblk:
        tgt = [((b // m) % 2 == 1) != rev for b in range(nblk)]
        levels.append(jnp.exp(jnp.concatenate([pb[b] if tgt[b] else tb[b] - pb[b] for b in range(nblk)], axis=0)))
        pb = [pb[b] + tb[b ^ m] if tgt[b] else pb[b] for b in range(nblk)]
        pair = {}
        for b in range(nblk):
            if min(b, b ^ m) not in pair:
                pair[min(b, b ^ m)] = tb[b] + tb[b ^ m]
        tb = [pair[min(b, b ^ m)] for b in range(nblk)]
        m *= 2
    p = jnp.concatenate(pb, axis=0)
    eb = jnp.exp(p)
    ek = jnp.exp(jnp.concatenate([tb[b] - pb[b] for b in range(nblk)], axis=0))
    etot = jnp.exp(tb[0][0:1, :])

    q16, k16 = q.astype(BF16), k.astype(BF16)
    levels16 = [lev.astype(BF16) for lev in levels]
    for hh in range(nheads):
        sl = slice(hh * hd, (hh + 1) * hd)
        qh, kh, vh = q[:, sl], k[:, sl], v[:, sl]
        a_mat = _dot_nt(q16[:, sl], k16[:, sl]) * masks_ref[d, 0]
        for li, lev in enumerate(levels16):
            eh = lev[:, sl]
            a_mat = a_mat + _dot_nt(q16[:, sl] * eh, k16[:, sl] * eh) * masks_ref[d, li + 1]
        st = st_ref[d, hh]
        o = _dot(a_mat.astype(BF16), vh.astype(BF16))
        o = o + _dot_nt((qh * eb[:, sl]).astype(BF16), st.astype(BF16))
        o_ref[0, :, sl] = o.astype(o_ref.dtype)
        khat = kh * ek[:, sl]
        st_ref[d, hh] = st * etot[:, sl] + _dot(vh.T.astype(BF16), khat.astype(BF16))


def _hgrn_kernel(qf_ref, if_ref, ff_ref, qb_ref, ib_ref, fb_ref, lbf_ref, lbb_ref, masks_ref,
                 of_ref, ob_ref, st_ref, *, c, nheads, hd):
    @pl.when(pl.program_id(1) == 0)
    def _():
        st_ref[...] = jnp.zeros_like(st_ref)

    ld = lambda ref: ref[0].astype(F32)
    _hgrn_dir(ld(qf_ref), ld(if_ref), ld(ff_ref), lbf_ref[...], masks_ref, 0, st_ref, of_ref, False, c, nheads, hd)
    _hgrn_dir(ld(qb_ref), ld(ib_ref), ld(fb_ref), lbb_ref[...], masks_ref, 1, st_ref, ob_ref, True, c, nheads, hd)


def _scan_chunk_maps(n_lat, n_ctx):
    n = n_lat + n_ctx
    fwd = lambda j: (j + n_lat) % n
    bwd = lambda j: n - 1 - j
    return n, fwd, bwd


def _hgrn_scan(u, cq, ci, cff, cfb, lbp_f, lbp_b, *, t_lat, c=CHUNK, nheads=A_HEADS, hd=A_DIM):
    b, r, _ = u.shape
    width = nheads * hd
    n, fwd, bwd = _scan_chunk_maps(t_lat // c, (r - t_lat) // c)
    masks = jnp.asarray(_level_masks(c))

    def spec(col, cm):
        return pl.BlockSpec((1, c, width), lambda bb, j: (bb, cm(j), col // width))

    const2 = pl.BlockSpec((3, width), lambda bb, j: (0, 0))
    out_shape = jax.ShapeDtypeStruct((b, r, width), BF16)
    return pl.pallas_call(
        functools.partial(_hgrn_kernel, c=c, nheads=nheads, hd=hd),
        grid=(b, n),
        in_specs=[spec(cq, fwd), spec(ci, fwd), spec(cff, fwd), spec(cq, bwd), spec(ci, bwd), spec(cfb, bwd),
                  const2, const2,
                  pl.BlockSpec(masks.shape, lambda bb, j: (0, 0, 0, 0))],
        out_specs=[pl.BlockSpec((1, c, width), lambda bb, j: (bb, fwd(j), 0)),
                   pl.BlockSpec((1, c, width), lambda bb, j: (bb, bwd(j), 0))],
        out_shape=[out_shape, out_shape],
        scratch_shapes=[pltpu.VMEM((2, nheads, hd, hd), F32)],
        compiler_params=_cp(("arbitrary", "arbitrary")),
        name="hgrn2_scan",
    )(u, u, u, u, u, u, lbp_f, lbp_b, masks)


def _ssd_dir(xm, bm, cm, dtc_raw, dtr_raw, prow, pcol, tri_ref, d, st_ref, o_ref, rev, c):
    nh = M_HEADS
    dt_c = _softplus(dtc_raw + prow[0:1, :])
    lf_c = dt_c * prow[1:2, :]
    dt_r = _softplus(dtr_raw + pcol[:, 0:1])[d * nh:(d + 1) * nh]
    lf_r = dt_r * pcol[d * nh:(d + 1) * nh, 1:2]
    lo, up = tri_ref[0], tri_ref[1]
    if rev:
        b_c = _dot_exact(up, lf_c)
        b_r = _dot_exact(lf_r, lo)
        causal = up > 0.5
        btot = b_c[0:1, :]
    else:
        b_c = _dot_exact(lo, lf_c)
        b_r = _dot_exact(lf_r, up)
        causal = lo > 0.5
        btot = b_c[c - 1:c, :]
    lane = lax.broadcasted_iota(jnp.int32, (1, LANE), 1)
    left = lane < M_HEADDIM
    bd_mask = jnp.concatenate([jnp.broadcast_to(left, (M_STATE, LANE)),
                               jnp.broadcast_to(jnp.logical_not(left), (M_STATE, LANE))], axis=0)
    for g in range(M_GROUPS):
        cg = cm[:, g * M_STATE:(g + 1) * M_STATE]
        bg_t = bm[:, g * M_STATE:(g + 1) * M_STATE].T
        gmat = _dot(cg.astype(BF16), bg_t.astype(BF16))
        for pp in range(nh // (2 * M_GROUPS)):
            pair = g * (nh // (2 * M_GROUPS)) + pp
            lhs, qs, kts, decs = [], [], [], []
            for h in (2 * pair, 2 * pair + 1):
                hc = d * nh + h
                bcol = b_c[:, hc:hc + 1]
                brow = b_r[h:h + 1, :]
                dtrow = dt_r[h:h + 1, :]
                dec = jnp.where(causal, jnp.exp(bcol - brow), 0.0) * dtrow
                lhs.append(gmat * dec)
                qs.append(cg * jnp.exp(bcol))
                bt = btot[:, hc:hc + 1]
                kts.append(bg_t * (jnp.exp(bt - brow) * dtrow))
                decs.append(jnp.broadcast_to(jnp.exp(bt), (M_STATE, LANE)))
            xp = xm[:, pair * LANE:(pair + 1) * LANE]
            rhs = jnp.concatenate([jnp.where(left, xp, 0.0), jnp.where(left, 0.0, xp)], axis=0)
            st = st_ref[d, pair]
            o = _dot(jnp.concatenate(lhs, axis=1).astype(BF16), rhs.astype(BF16))
            o = o + _dot(jnp.concatenate(qs, axis=1).astype(BF16), st.astype(BF16))
            o_ref[0, :, pair * LANE:(pair + 1) * LANE] = o.astype(o_ref.dtype)
            upd = _dot(jnp.concatenate(kts, axis=0).astype(BF16), xp.astype(BF16))
            st_ref[d, pair] = st * jnp.concatenate(decs, axis=0) + jnp.where(bd_mask, upd, 0.0)


def _ssd_kernel(xf_ref, bf_ref, cf_ref, dcf_ref, drf_ref, xb_ref, bb_ref, cb_ref, dcb_ref, drb_ref,
                prow_ref, pcol_ref, tri_ref, of_ref, ob_ref, st_ref, *, c):
    @pl.when(pl.program_id(1) == 0)
    def _():
        st_ref[...] = jnp.zeros_like(st_ref)

    ld = lambda ref: ref[0].astype(F32)
    _ssd_dir(ld(xf_ref), ld(bf_ref), ld(cf_ref), ld(dcf_ref), drf_ref[0], prow_ref[...], pcol_ref[...],
             tri_ref, 0, st_ref, of_ref, False, c)
    _ssd_dir(ld(xb_ref), ld(bb_ref), ld(cb_ref), ld(dcb_ref), drb_ref[0], prow_ref[...], pcol_ref[...],
             tri_ref, 1, st_ref, ob_ref, True, c)


def _tri_mats(c):
    t = np.arange(c)[:, None]
    s = np.arange(c)[None, :]
    return np.stack([(s <= t), (s >= t)]).astype(np.float32)


def _ssd_scan(xbc, u, dt_t, prow, pcol, *, t_lat, c=CHUNK):
    b, r, _ = xbc.shape
    n, fwd, bwd = _scan_chunk_maps(t_lat // c, (r - t_lat) // c)
    gs = M_GROUPS * M_STATE
    tri = jnp.asarray(_tri_mats(c))

    def specs(cm):
        return [pl.BlockSpec((1, c, M_INNER), lambda bb, j: (bb, cm(j), 0)),
                pl.BlockSpec((1, c, gs), lambda bb, j: (bb, cm(j), M_INNER // gs)),
                pl.BlockSpec((1, c, gs), lambda bb, j: (bb, cm(j), M_INNER // gs + 1)),
                pl.BlockSpec((1, c, LANE), lambda bb, j: (bb, cm(j), U_DT // LANE)),
                pl.BlockSpec((1, 2 * M_HEADS, c), lambda bb, j: (bb, 0, cm(j)))]

    out_shape = jax.ShapeDtypeStruct((b, r, M_INNER), BF16)
    return pl.pallas_call(
        functools.partial(_ssd_kernel, c=c),
        grid=(b, n),
        in_specs=specs(fwd) + specs(bwd) + [
            pl.BlockSpec(prow.shape, lambda bb, j: (0, 0)),
            pl.BlockSpec(pcol.shape, lambda bb, j: (0, 0)),
            pl.BlockSpec(tri.shape, lambda bb, j: (0, 0, 0))],
        out_specs=[pl.BlockSpec((1, c, M_INNER), lambda bb, j: (bb, fwd(j), 0)),
                   pl.BlockSpec((1, c, M_INNER), lambda bb, j: (bb, bwd(j), 0))],
        out_shape=[out_shape, out_shape],
        scratch_shapes=[pltpu.VMEM((2, M_HEADS // 2, 2 * M_STATE, LANE), F32)],
        compiler_params=_cp(("arbitrary", "arbitrary")),
        name="ssd_scan",
    )(xbc, xbc, xbc, u, dt_t, xbc, xbc, xbc, u, dt_t, prow, pcol, tri)


def _band_mats(windows, tb, period):
    t = np.arange(tb)[:, None]
    s = np.arange(tb)[None, :]
    out = []
    for w in windows:
        left = w // 2
        right = w - 1 - left
        out.append((t // period == s // period) & (s >= t - left) & (s <= t + right))
    return np.stack(out).astype(np.float32)


def _clip_count(idx, n, left, right):
    return jnp.minimum(idx + right + 1, n) - jnp.maximum(idx - left, 0)


def _pool_kernel(z_ref, band_ref, gw_ref, sc_ref, o_ref, pad_ref, *, n, tb, grid_w, windows):
    g = pl.program_id(1)
    band = band_ref[0].astype(BF16)
    shift = int(math.log2(grid_w)) if grid_w else 0
    for gi, w in enumerate(windows):
        @pl.when(g == gi)
        def _(w=w):
            left = w // 2
            right = w - 1 - left
            wr = w if grid_w else 1
            top = left * grid_w
            if grid_w:
                pad_ref[0:top, :] = jnp.zeros((top, LANE), F32)
                if right:
                    pad_ref[top + n:top + n + right * grid_w, :] = jnp.zeros((right * grid_w, LANE), F32)

            def col_sum(i, carry):
                pad_ref[pl.ds(top + i * tb, tb), :] = _dot(band, z_ref[0, pl.ds(i * tb, tb), :].astype(BF16))
                return carry

            unroll = min(4, n // tb)
            lax.fori_loop(0, n // tb, col_sum, 0, unroll=unroll)

            def finish(i, carry):
                acc = pad_ref[pl.ds(i * tb, tb), :]
                for kk in range(1, wr):
                    acc = acc + pad_ref[pl.ds(i * tb + kk * grid_w, tb), :]
                t = i * tb + lax.broadcasted_iota(jnp.int32, (tb, 1), 0)
                if grid_w:
                    cnt = (_clip_count(lax.shift_right_logical(t, shift), n // grid_w, left, right)
                           * _clip_count(t & (grid_w - 1), grid_w, left, right))
                else:
                    cnt = _clip_count(t, n, left, right)
                y = acc / cnt.astype(F32) - z_ref[0, pl.ds(i * tb, tb), :].astype(F32)
                out = _dot(y.astype(BF16), gw_ref[0].astype(BF16)) * sc_ref[0]
                o_ref[0, pl.ds(i * tb, tb), :] = out.astype(o_ref.dtype)
                return carry

            lax.fori_loop(0, n // tb, finish, 0, unroll=unroll)


def _pool(u, group_w, scale, *, row0, n, grid_w):
    b = u.shape[0]
    tb = 2 * grid_w if grid_w else n
    bands = jnp.asarray(_band_mats(P_WINDOWS, tb, grid_w if grid_w else n))
    rb = row0 // n
    pad_rows = n + (max(P_WINDOWS) - 1) * grid_w
    return pl.pallas_call(
        functools.partial(_pool_kernel, n=n, tb=tb, grid_w=grid_w, windows=P_WINDOWS),
        grid=(b, len(P_WINDOWS)),
        in_specs=[
            pl.BlockSpec((1, n, P_GROUP), lambda bb, g: (bb, rb, U_P // P_GROUP + g)),
            pl.BlockSpec((1, tb, tb), lambda bb, g: (g, 0, 0)),
            pl.BlockSpec((1, P_GROUP, P_GROUP), lambda bb, g: (g, 0, 0)),
            pl.BlockSpec((1, 1, P_GROUP), lambda bb, g: (g, 0, 0)),
        ],
        out_specs=pl.BlockSpec((1, n, P_GROUP), lambda bb, g: (bb, 0, g)),
        out_shape=jax.ShapeDtypeStruct((b, n, P_WIDTH), BF16),
        scratch_shapes=[pltpu.VMEM((pad_rows, LANE), F32)],
        compiler_params=_cp(("arbitrary", "arbitrary")),
        name="pool_grid" if grid_w else "pool_seq",
    )(u, bands, group_w, scale.reshape(len(P_WINDOWS), 1, P_GROUP))


def _chan_dft_mats():
    k = np.arange(F_WIDTH // 4)
    ang = 2.0 * np.pi * ((k[:, None] * k[None, :]) % len(k)) / len(k)
    eye = np.eye(4)
    w = np.concatenate([np.kron(eye, np.cos(ang)), np.kron(eye, np.sin(ang))], axis=1)
    hi = w.astype(np.float32).astype(BF16)
    lo = (w - hi.astype(np.float64)).astype(np.float32).astype(BF16)
    return jnp.asarray(hi), jnp.asarray(lo)


def _chan_dft_kernel(z_ref, wh_ref, wl_ref, o_ref):
    z = z_ref[0]
    o_ref[0] = (_dot(z, wh_ref[...]) + _dot(z, wl_ref[...])).astype(o_ref.dtype)


def _chan_dft(u, *, tm):
    b, r, _ = u.shape
    wh, wl = _chan_dft_mats()
    wspec = pl.BlockSpec(wh.shape, lambda bb, i: (0, 0))
    return pl.pallas_call(
        _chan_dft_kernel,
        grid=(b, r // tm),
        in_specs=[pl.BlockSpec((1, tm, F_WIDTH), lambda bb, i: (bb, i, U_F // F_WIDTH)), wspec, wspec],
        out_specs=pl.BlockSpec((1, tm, 2 * F_WIDTH), lambda bb, i: (bb, i, 0)),
        out_shape=jax.ShapeDtypeStruct((b, r, 2 * F_WIDTH), BF16),
        compiler_params=_cp(("arbitrary", "arbitrary")),
        name="chan_dft",
    )(u, wh, wl)


def _dft_cols(n, ncols):
    k = jnp.arange(n, dtype=jnp.int32)[:, None]
    j = jnp.arange(ncols, dtype=jnp.int32)[None, :]
    ang = ((k * j) & (n - 1)).astype(F32) * (2.0 * math.pi / n)
    return jnp.cos(ang), jnp.sin(ang)


ROT_PERIOD = 16


def _rot_patterns(n, tk):
    assert ROT_PERIOD % (n // tk) == 0
    k = np.arange(ROT_PERIOD)[None, :, None]
    t0 = (np.arange(n // tk) * tk)[:, None, None]
    ang = 2.0 * np.pi * ((k * t0) % n) / n
    shape = (n // tk, ROT_PERIOD, tk)
    return (jnp.asarray(np.broadcast_to(np.cos(ang), shape), F32), jnp.asarray(np.broadcast_to(np.sin(ang), shape), F32))


def _fold_mats(tf):
    flip = np.zeros((tf, tf), np.float32)
    flip[np.arange(1, tf), tf - np.arange(1, tf)] = 1.0
    first = np.zeros((tf, PACK), np.float32)
    first[0, 0] = 1.0
    return jnp.asarray(flip, BF16), jnp.asarray(first, BF16)


def _dft_fold_kernel(cur_ref, mir_ref, nxt_ref, flip_ref, first_ref, o_ref):
    tf = cur_ref.shape[1]
    mirrored = _dot(flip_ref[...], mir_ref[0]) + _dot(first_ref[...], nxt_ref[0])
    row = pl.program_id(1) * tf + lax.broadcasted_iota(jnp.int32, (tf, 1), 0)
    mirrored = jnp.where(row == 0, 0.0, mirrored)
    cur = cur_ref[0].astype(F32)
    sign = jnp.where(lax.broadcasted_iota(jnp.int32, (1, 2 * F_WIDTH), 1) < F_WIDTH, 1.0, -1.0)
    o_ref[0] = (cur + sign * mirrored).astype(o_ref.dtype)


def _dft_fold(zcs, *, n, tf=512):
    b, r, w = zcs.shape
    nt = n // tf
    flip, first = _fold_mats(tf)
    last16 = r // PACK - 1
    return pl.pallas_call(
        _dft_fold_kernel,
        grid=(b, nt // 2),
        in_specs=[pl.BlockSpec((1, tf, w), lambda bb, i: (bb, i, 0)),
                  pl.BlockSpec((1, tf, w), lambda bb, i: (bb, nt - 1 - i, 0)),
                  pl.BlockSpec((1, PACK, w), lambda bb, i: (bb, jnp.minimum((nt - i) * (tf // PACK), last16), 0)),
                  pl.BlockSpec(flip.shape, lambda bb, i: (0, 0)),
                  pl.BlockSpec(first.shape, lambda bb, i: (0, 0))],
        out_specs=pl.BlockSpec((1, tf, w), lambda bb, i: (bb, i, 0)),
        out_shape=jax.ShapeDtypeStruct((b, n // 2, w), BF16),
        compiler_params=_cp(("arbitrary", "arbitrary")),
        name="dft_fold",
    )(zcs, zcs, zcs, flip, first)


def _time_dft_kernel(cb_ref, sb_ref, pc_ref, ps_ref, zc_ref, zs_ref, zh_ref, o_ref, acc_ref,
                     *, nb, tm, tk, scale, folded):
    kk = pl.program_id(1)

    @pl.when(kk == 0)
    def _():
        acc_ref[...] = jnp.zeros_like(acc_ref)

    cb = cb_ref[...].reshape(tm // ROT_PERIOD, ROT_PERIOD, tk)
    sb = sb_ref[...].reshape(tm // ROT_PERIOD, ROT_PERIOD, tk)
    ca, sa = pc_ref[...], ps_ref[...]
    ct = (ca * cb - sa * sb).reshape(tm, tk).astype(BF16)
    st = (sa * cb + ca * sb).reshape(tm, tk).astype(BF16)
    for bb in range(nb):
        acc_ref[bb] += _dot(ct, zc_ref[bb]) - _dot(st, zs_ref[bb])

    @pl.when(kk == pl.num_programs(1) - 1)
    def _():
        y = acc_ref[...]
        if folded:
            row = lax.broadcasted_iota(jnp.int32, (1, tm, 1), 1)
            y = y + jnp.where((row & 1) == 0, 1.0, -1.0) * zh_ref[:, 0:1, :].astype(F32)
        o_ref[...] = (y * scale).astype(o_ref.dtype)


def _time_dft(zcs, *, row0, n, tm, tk, folded):
    b = zcs.shape[0]
    scale = 1.0 / math.sqrt(n * (F_WIDTH // 4))
    cb, sb = _dft_cols(n, tk)
    pc, ps = _rot_patterns(n, tk)
    pat = pl.BlockSpec((None, ROT_PERIOD, tk), lambda i, k: (k, 0, 0))
    z = _dft_fold(zcs, n=n) if folded else zcs
    zrow = 0 if folded else row0 // tk
    return pl.pallas_call(
        functools.partial(_time_dft_kernel, nb=b, tm=tm, tk=tk, scale=scale, folded=folded),
        grid=(n // tm, (n // 2 if folded else n) // tk),
        in_specs=[
            pl.BlockSpec((tm, tk), lambda i, k: (i, 0)),
            pl.BlockSpec((tm, tk), lambda i, k: (i, 0)),
            pat, pat,
            pl.BlockSpec((b, tk, F_WIDTH), lambda i, k: (0, zrow + k, 0)),
            pl.BlockSpec((b, tk, F_WIDTH), lambda i, k: (0, zrow + k, 1)),
            pl.BlockSpec((b, PACK, F_WIDTH), lambda i, k: (0, (row0 + n // 2) // PACK, 0)),
        ],
        out_specs=pl.BlockSpec((b, tm, F_WIDTH), lambda i, k: (0, i, 0)),
        out_shape=jax.ShapeDtypeStruct((b, n, F_WIDTH), BF16),
        scratch_shapes=[pltpu.VMEM((b, tm, F_WIDTH), F32)],
        compiler_params=_cp(("arbitrary", "arbitrary")),
        name="time_dft_%d" % n,
    )(cb, sb, pc, ps, z, z, zcs)


def _merge_kernel(oaf_ref, oab_ref, g_ref, omf_ref, omb_ref, xm_ref, z0_ref, z1_ref, pm_ref, fm_ref, pmc_ref, fmc_ref,
                  anw_ref, dsk_ref, mnw_ref, wbr_ref, g0_ref, g1_ref, g2_ref, g3_ref, o_ref, act_ref):
    @pl.when(pl.program_id(2) == 0)
    def _():
        ld = lambda ref: ref[0].astype(F32)
        oa = ld(oaf_ref) + ld(oab_ref)
        gate = _silu(ld(g_ref))
        for hh in range(A_HEADS):
            sl = slice(hh * A_DIM, (hh + 1) * A_DIM)
            oh = oa[:, sl]
            ms = jnp.mean(oh * oh, axis=-1, keepdims=True)
            act_ref[:, sl] = (oh * lax.rsqrt(ms + EPS) * anw_ref[...] * gate[:, sl]).astype(BF16)
        y = ld(omf_ref) + ld(omb_ref) + dsk_ref[...] * ld(xm_ref)
        z = jnp.concatenate([ld(z0_ref), ld(z1_ref)], axis=1)
        y = y * _silu(z)
        ms = jnp.mean(y * y, axis=-1, keepdims=True)
        act_ref[:, A_WIDTH:A_WIDTH + M_INNER] = (y * lax.rsqrt(ms + EPS) * mnw_ref[...]).astype(BF16)
        act_ref[:, A_WIDTH + M_INNER:A_WIDTH + M_INNER + P_WIDTH] = pm_ref[0].astype(BF16)
        act_ref[:, A_WIDTH + M_INNER + P_WIDTH:] = fm_ref[0].astype(BF16)

        @pl.when(pl.program_id(1) == pl.num_programs(1) - 1)
        def _():
            t_ctx = pmc_ref.shape[1]
            tail = slice(act_ref.shape[0] - t_ctx, act_ref.shape[0])
            act_ref[tail, A_WIDTH + M_INNER:A_WIDTH + M_INNER + P_WIDTH] = pmc_ref[0].astype(BF16)
            act_ref[tail, A_WIDTH + M_INNER + P_WIDTH:] = fmc_ref[0].astype(BF16)

    offs = (0, A_WIDTH, A_WIDTH + M_INNER, A_WIDTH + M_INNER + P_WIDTH, ACT_W)
    acc = None
    for br, gr in enumerate((g0_ref, g1_ref, g2_ref, g3_ref)):
        yb = _dot(act_ref[:, offs[br]:offs[br + 1]], wbr_ref[offs[br]:offs[br + 1], :])
        term = _sigmoid(gr[0].astype(F32)) * yb
        acc = term if acc is None else acc + term
    o_ref[0] = acc.astype(o_ref.dtype)


def _merge(u, oaf, oab, omf, omb, xbc, pm, fm, pmc, fmc, anw, dsk, mnw, wbr, *, tm, tn=512):
    b, r, _ = u.shape
    t_ctx = pmc.shape[1]
    assert r - pm.shape[1] == t_ctx and t_ctx <= tm and r % tm == 0
    ctx_rows = pl.BlockSpec((1, t_ctx, P_WIDTH), lambda bb, i, j: (bb, 0, 0))

    def rows(width, cb):
        return pl.BlockSpec((1, tm, width), lambda bb, i, j: (bb, i, cb))

    def vec(width):
        return pl.BlockSpec((1, width), lambda bb, i, j: (0, 0))

    def gate(br):
        return pl.BlockSpec((1, tm, tn), lambda bb, i, j: (bb, i, (U_GATE + br * D_MODEL) // tn + j))

    return pl.pallas_call(
        _merge_kernel,
        grid=(b, r // tm, D_MODEL // tn),
        in_specs=[rows(A_WIDTH, 0), rows(A_WIDTH, 0), rows(A_WIDTH, U_G // A_WIDTH),
                  rows(M_INNER, 0), rows(M_INNER, 0), rows(M_INNER, 0),
                  rows(512, U_Z // 512), rows(512, U_Z // 512 + 1),
                  rows(P_WIDTH, 0), rows(F_WIDTH, 0), ctx_rows, ctx_rows,
                  vec(A_DIM), vec(M_INNER), vec(M_INNER),
                  pl.BlockSpec((ACT_W, tn), lambda bb, i, j: (0, j)),
                  gate(0), gate(1), gate(2), gate(3)],
        out_specs=pl.BlockSpec((1, tm, tn), lambda bb, i, j: (bb, i, j)),
        out_shape=jax.ShapeDtypeStruct((b, r, D_MODEL), BF16),
        scratch_shapes=[pltpu.VMEM((tm, ACT_W), BF16)],
        compiler_params=_cp(("arbitrary", "arbitrary", "arbitrary")),
        name="merge",
    )(oaf, oab, u, omf, omb, xbc, u, u, pm, fm, pmc, fmc, anw, dsk, mnw, wbr, u, u, u, u)


def _proj_resid_kernel(y_ref, w_ref, x_ref, gl_ref, gc_ref, o_ref, *, tm, t_lat):
    is_ctx = _row_is_ctx(pl.program_id(1) * tm, tm, t_lat)
    gate = jnp.where(is_ctx, gc_ref[...], gl_ref[0])
    o_ref[0] = x_ref[0] + gate * _dot(y_ref[0], w_ref[...])


def _proj_resid(y, w, x, gl, gc, *, t_lat, tm):
    b, r, d = x.shape
    k = y.shape[2]
    return pl.pallas_call(
        functools.partial(_proj_resid_kernel, tm=tm, t_lat=t_lat),
        grid=(b, r // tm),
        in_specs=[pl.BlockSpec((1, tm, k), lambda bb, i: (bb, i, 0)),
                  pl.BlockSpec((k, d), lambda bb, i: (0, 0)),
                  pl.BlockSpec((1, tm, d), lambda bb, i: (bb, i, 0)),
                  pl.BlockSpec((1, 1, d), lambda bb, i: (bb, 0, 0)),
                  pl.BlockSpec((1, d), lambda bb, i: (0, 0))],
        out_specs=pl.BlockSpec((1, tm, d), lambda bb, i: (bb, i, 0)),
        out_shape=jax.ShapeDtypeStruct((b, r, d), F32),
        compiler_params=_cp(("arbitrary", "arbitrary")),
        name="proj_resid",
    )(y, w, x, gl, gc)


def _conv3(ext, s_ref, w, bias, tm, m_prev, m_next):
    main = ext[0:tm]
    s_ref[SUBLANE:SUBLANE + tm, :] = main
    s_ref[SUBLANE - 1:SUBLANE, :] = ext[tm + SUBLANE - 1:tm + SUBLANE]
    s_ref[SUBLANE + tm:SUBLANE + tm + 1, :] = ext[tm + SUBLANE:tm + SUBLANE + 1]
    prev = s_ref[pl.ds(SUBLANE - 1, tm), :]
    nxt = s_ref[pl.ds(SUBLANE + 1, tm), :]
    return main * w[1:2, :] + (prev * m_prev) * w[0:1, :] + (nxt * m_next) * w[2:3, :] + bias


def _ffn_kernel(xp_ref, x_ref, xn_ref, nw_ref, shl_ref, scl_ref, shc_ref, scc_ref, gl_ref, gc_ref, fnw_ref,
                wa_ref, wb_ref, cwa_ref, cwb_ref, ba_ref, bb_ref, wd_ref, o_ref, h_ref, sa_ref, sb_ref,
                *, tm, t_lat, r_tot, final_norm):
    i = pl.program_id(1)
    kk = pl.program_id(2)
    row0 = i * tm

    @pl.when(kk == 0)
    def _():
        vecs = (nw_ref[...], shl_ref[0], scl_ref[0], shc_ref[...], scc_ref[...])
        _norm_mod_tile(x_ref, h_ref, tm, row0, t_lat, *vecs)
        halo = jnp.concatenate([xp_ref[0], xn_ref[0]], axis=0)
        hrow = lax.broadcasted_iota(jnp.int32, (2 * SUBLANE, 1), 0)
        grow = jnp.where(hrow < SUBLANE, row0 - SUBLANE + hrow, row0 + tm - SUBLANE + hrow)
        h_ref[tm:tm + 2 * SUBLANE, :] = _norm_mod(halo, grow >= t_lat, *vecs).astype(BF16)
        o_ref[...] = jnp.zeros_like(o_ref)

    t = row0 + lax.broadcasted_iota(jnp.int32, (tm, 1), 0)
    m_prev = jnp.where((t == 0) | (t == t_lat), 0.0, 1.0)
    m_next = jnp.where((t == t_lat - 1) | (t == r_tot - 1), 0.0, 1.0)
    h = h_ref[...]
    a = _conv3(_dot(h, wa_ref[...]), sa_ref, cwa_ref[...], ba_ref[...], tm, m_prev, m_next)
    g = _conv3(_dot(h, wb_ref[...]), sb_ref, cwb_ref[...], bb_ref[...], tm, m_prev, m_next)
    o_ref[0] += _dot((_silu(a) * g).astype(BF16), wd_ref[...])

    @pl.when(kk == pl.num_programs(2) - 1)
    def _():
        gate = jnp.where(_row_is_ctx(row0, tm, t_lat), gc_ref[...], gl_ref[0])
        y = x_ref[0] + gate * o_ref[0]
        if final_norm:
            ms = jnp.mean(y * y, axis=-1, keepdims=True)
            y = y * lax.rsqrt(ms + EPS) * fnw_ref[...]
        o_ref[0] = y


def _ffn(x, nw, shl, scl, shc, scc, gl, gc, fnw, w_up, cw, cb, wd, *, t_lat, tm, tk, final_norm, out_rows):
    b, r, d = x.shape
    nk = D_FF // tk
    x_prev, x_next = _halo_specs(tm, d, lambda k: 0, r, SUBLANE)
    vec_l = pl.BlockSpec((1, 1, d), lambda bb, i, k: (bb, 0, 0))
    vec_c = pl.BlockSpec((1, d), lambda bb, i, k: (0, 0))
    cb2 = cb.reshape(1, 2 * D_FF)
    return pl.pallas_call(
        functools.partial(_ffn_kernel, tm=tm, t_lat=t_lat, r_tot=r, final_norm=final_norm),
        grid=(b, r // tm, nk),
        in_specs=[x_prev, pl.BlockSpec((1, tm, d), lambda bb, i, k: (bb, i, 0)), x_next,
                  vec_c, vec_l, vec_l, vec_c, vec_c, vec_l, vec_c, vec_c,
                  pl.BlockSpec((d, tk), lambda bb, i, k: (0, k)),
                  pl.BlockSpec((d, tk), lambda bb, i, k: (0, nk + k)),
                  pl.BlockSpec((FFN_CONV, tk), lambda bb, i, k: (0, k)),
                  pl.BlockSpec((FFN_CONV, tk), lambda bb, i, k: (0, nk + k)),
                  pl.BlockSpec((1, tk), lambda bb, i, k: (0, k)),
                  pl.BlockSpec((1, tk), lambda bb, i, k: (0, nk + k)),
                  pl.BlockSpec((tk, d), lambda bb, i, k: (k, 0))],
        out_specs=pl.BlockSpec((1, tm, d), lambda bb, i, k: (bb, i, 0)),
        out_shape=jax.ShapeDtypeStruct((b, out_rows, d), F32),
        scratch_shapes=[pltpu.VMEM((tm + 2 * SUBLANE, d), BF16),
                        pltpu.VMEM((tm + 2 * SUBLANE, tk), F32),
                        pltpu.VMEM((tm + 2 * SUBLANE, tk), F32)],
        compiler_params=_cp(("arbitrary", "arbitrary", "arbitrary")),
        name="ffn",
    )(x, x, x, nw, shl, scl, shc, scc, gl, gc, fnw, w_up, w_up, cw, cw, cb2, cb2, wd)


def _arrange_w_in(w):
    pad = jnp.zeros((w.shape[0], NU - W_END), w.dtype)
    return jnp.concatenate([w[:, W_GATE:W_END], w[:, W_A:W_DT], w[:, W_P:W_GATE], w[:, W_DT:W_P], pad],
                           axis=1).astype(BF16)


def kernel(x, c, ctx, c_ctx, w_ada, b_ada, norm1_w, w_in, a_lb_logits, a_norm_w, w_br_a, m_conv_w, m_conv_b,
           m_dt_bias, m_a_log, m_d, m_norm_w, w_br_m, p_group_w, p_scale, w_br_p, w_br_f, w_out, norm2_w,
           w_up, ffn_conv_w, ffn_conv_b, w_down, final_norm_w):
    nb, t_lat, d = x.shape
    t_ctx = ctx.shape[1]
    depth = w_ada.shape[0]
    tm_in, tm = 1056, 768
    assert t_lat % CHUNK == 0 and t_ctx % CHUNK == 0 and (t_lat + t_ctx) % tm_in == 0 and (t_lat + t_ctx) % tm == 0

    lb_all = jnp.cumsum(jax.nn.softmax(a_lb_logits.astype(F32), axis=0), axis=0)
    lb_all = lb_all - lb_all[0]

    xs = jnp.concatenate([x, ctx], axis=1)
    c_all = jnp.concatenate([c, c_ctx[None, :], jnp.zeros((SUBLANE - nb - 1, d), F32)], axis=0)
    mod = _ada(c_all, w_ada, b_ada)

    for l in range(depth):
        last = l == depth - 1
        ml = mod[l, :nb].reshape(nb, 1, 6, d)
        mc = mod[l, nb].reshape(6, d)
        lat = lambda k: ml[:, :, k, :]
        cx = lambda k: mc[k:k + 1, :]

        u = _norm_mod_matmul(xs, norm1_w[l][None, :], lat(0), lat(1), cx(0), cx(1), _arrange_w_in(w_in[l]),
                             t_lat=t_lat, tm=tm_in, tn=2432)
        lbp = [jnp.stack([jnp.log(lb_all[l, dd]), jnp.log1p(-lb_all[l, dd]), 1.0 - lb_all[l, dd]])
               for dd in range(2)]
        oaf, oab = _hgrn_scan(u, U_Q, U_I, U_FF, U_FB, lbp[0], lbp[1], t_lat=t_lat)

        xbc = _conv_silu(u, U_XBC, M_XBC, m_conv_w[l], m_conv_b[l], t_lat=t_lat, tm=tm)
        dt_t = jnp.transpose(u[:, :, U_DT:U_DT + 2 * M_HEADS].astype(F32), (0, 2, 1))
        a_neg = -jnp.exp(m_a_log[l].astype(F32)).reshape(-1)
        dtb = m_dt_bias[l].astype(F32).reshape(-1)
        prow = jnp.zeros((2, LANE), F32).at[0, :2 * M_HEADS].set(dtb).at[1, :2 * M_HEADS].set(a_neg)
        pcol = jnp.stack([dtb, a_neg], axis=1)
        omf, omb = _ssd_scan(xbc, u, dt_t, prow, pcol, t_lat=t_lat)

        pm = _pool(u, p_group_w[l], p_scale[l], row0=0, n=t_lat, grid_w=GRID_W)
        pmc = _pool(u, p_group_w[l], p_scale[l], row0=t_lat, n=t_ctx, grid_w=0)
        zcs = _chan_dft(u, tm=tm)
        fm = _time_dft(zcs, row0=0, n=t_lat, tm=1024, tk=512, folded=True)
        fmc = _time_dft(zcs, row0=t_lat, n=t_ctx, tm=t_ctx, tk=t_ctx, folded=False)

        wbr = jnp.concatenate([w_br_a[l], w_br_m[l], w_br_p[l], w_br_f[l]], axis=0).astype(BF16)
        dsk = jnp.repeat(m_d[l].astype(F32), M_HEADDIM)[None, :]
        anw = a_norm_w[l].astype(F32)[None, :]
        y = _merge(u, oaf, oab, omf, omb, xbc, pm, fm, pmc, fmc, anw, dsk, m_norm_w[l][None, :], wbr, tm=tm)
        xs = _proj_resid(y, w_out[l].astype(BF16), xs, lat(2), cx(2), t_lat=t_lat, tm=tm)

        xs = _ffn(xs, norm2_w[l][None, :], lat(3), lat(4), cx(3), cx(4), lat(5), cx(5), final_norm_w[None, :],
                  w_up[l].astype(BF16), ffn_conv_w[l], ffn_conv_b[l], w_down[l].astype(BF16),
                  t_lat=t_lat, tm=tm, tk=512, final_norm=last, out_rows=t_lat if last else t_lat + t_ctx)
    return xs
```
